```python
import math
import jax, jax.numpy as jnp
from jax import lax
import numpy as np

D_MODEL = 1024
BATCH = 16
SEQ = 2048
DEPTH = 2

CHUNK = 64
N_MIXERS = 2
D_FF = 2816
PLE_DIM = 256
S5_GROUP = 16
S5_GROUPS = D_MODEL // S5_GROUP
S5_STATE = 64
SB_HEAD_DIM = 64
SB_HEADS = D_MODEL // SB_HEAD_DIM
Q_BLOCK = 128
N_A = (DEPTH + 1) // 2
N_B = DEPTH // 2
EPS = 1e-6
DT_MIN = 1e-3
DT_MAX = 1e-1

kernel_name = 'hybrid_s5_stickbreaking_macaron'


def rmsnorm(x, g):
    xf = x.astype(jnp.float32)
    y = xf * lax.rsqrt(jnp.mean(xf * xf, axis=-1, keepdims=True) + EPS)
    return (y * g.astype(jnp.float32)).astype(x.dtype)


def swiglu(x, w1, w3, w2):
    return (jax.nn.silu(x @ w1) * (x @ w3)) @ w2


def _cmul(ar, ai, br, bi):
    return ar * br - ai * bi, ar * bi + ai * br


def _s5_combine(e1, e2):
    a1r, a1i, b1r, b1i = e1
    a2r, a2i, b2r, b2i = e2
    ar, ai = _cmul(a2r, a2i, a1r, a1i)
    cr, ci = _cmul(a2r, a2i, b1r, b1i)
    return ar, ai, cr + b2r, ci + b2i


def s5_mixer(h, w_in, a_re, a_im, log_dt, b_re, b_im, c_re, c_im, d_skip, w_glu):
    bsz, seq_len, _ = h.shape
    f32 = jnp.float32
    u = h @ w_in
    ug = u.astype(f32).reshape(bsz, seq_len, S5_GROUPS, S5_GROUP)
    lam_re = jnp.minimum(a_re.astype(f32), -1e-4)
    lam_im = a_im.astype(f32)
    dt = jnp.exp(log_dt.astype(f32))[:, None]
    mag = jnp.exp(lam_re * dt)
    abar_re = mag * jnp.cos(lam_im * dt)
    abar_im = mag * jnp.sin(lam_im * dt)
    den = lam_re * lam_re + lam_im * lam_im
    nr = abar_re - 1.0
    ni = abar_im
    fr = (nr * lam_re + ni * lam_im) / den
    fi = (ni * lam_re - nr * lam_im) / den
    bre = b_re.astype(f32)
    bim = b_im.astype(f32)
    bbar_re = fr[..., None] * bre - fi[..., None] * bim
    bbar_im = fr[..., None] * bim + fi[..., None] * bre
    bu_re = jnp.einsum('blgh,gph->blgp', ug, bbar_re)
    bu_im = jnp.einsum('blgh,gph->blgp', ug, bbar_im)
    a_seq_re = jnp.broadcast_to(abar_re, (1, seq_len, S5_GROUPS, S5_STATE))
    a_seq_im = jnp.broadcast_to(abar_im, (1, seq_len, S5_GROUPS, S5_STATE))
    _, _, s_re, s_im = lax.associative_scan(
        _s5_combine, (a_seq_re, a_seq_im, bu_re, bu_im), axis=1)
    y = (jnp.einsum('blgp,ghp->blgh', s_re, c_re.astype(f32))
         - jnp.einsum('blgp,ghp->blgh', s_im, c_im.astype(f32)))
    y = y.reshape(bsz, seq_len, D_MODEL) + d_skip.astype(f32) * u.astype(f32)
    z = jax.nn.gelu(y).astype(h.dtype)
    z_out, z_gate = jnp.split(z @ w_glu, 2, axis=-1)
    return z_out * jax.nn.sigmoid(z_gate)


def stick_breaking_mixer(h, w_qkv, w_o):
    bsz, seq_len, _ = h.shape
    qkv = (h @ w_qkv).reshape(bsz, seq_len, 3, SB_HEADS, SB_HEAD_DIM)
    q = qkv[:, :, 0].transpose(0, 2, 1, 3)
    k = qkv[:, :, 1].transpose(0, 2, 1, 3)
    v = qkv[:, :, 2].transpose(0, 2, 1, 3)
    scale = SB_HEAD_DIM ** -0.5
    outs = []
    for blk in range(seq_len // Q_BLOCK):
        start = blk * Q_BLOCK
        end = start + Q_BLOCK
        qb = q[:, :, start:end]
        kb = k[:, :, :end]
        vb = v[:, :, :end]
        z = jnp.einsum('bhqd,bhkd->bhqk', qb, kb).astype(jnp.float32) * scale
        t_pos = start + jnp.arange(Q_BLOCK)[:, None]
        s_pos = jnp.arange(end)[None, :]
        strict = s_pos < t_pos
        log_keep = jnp.where(strict, jax.nn.log_sigmoid(-z), 0.0)
        later = lax.cumsum(log_keep, axis=3, reverse=True) - log_keep
        att = jnp.where(strict, jnp.exp(jax.nn.log_sigmoid(z) + later), 0.0)
        outs.append(jnp.einsum('bhqk,bhkd->bhqd', att.astype(vb.dtype), vb))
    o = jnp.concatenate(outs, axis=2).transpose(0, 2, 1, 3).reshape(bsz, seq_len, D_MODEL)
    return o @ w_o


def setup_inputs(seed: int = 0) -> dict:
    key = jax.random.key(seed)
    ks = jax.random.split(key, 32)
    f32 = jnp.float32
    nrm = lambda k, shape, s: jax.random.normal(k, shape, f32) * s
    gain = lambda k, shape: 1.0 + 0.02 * jax.random.normal(k, shape, f32)
    d_in = D_MODEL ** -0.5
    a_im0 = math.pi * jnp.arange(S5_STATE, dtype=f32)
    return {
        'x': jax.random.normal(ks[0], (BATCH, SEQ, D_MODEL), f32),
        'p': jax.random.normal(ks[1], (DEPTH, BATCH, SEQ, PLE_DIM), f32),
        'ffn1_norm': gain(ks[2], (DEPTH, D_MODEL)),
        'ffn1_w1': nrm(ks[3], (DEPTH, D_MODEL, D_FF), d_in),
        'ffn1_w3': nrm(ks[4], (DEPTH, D_MODEL, D_FF), d_in),
        'ffn1_w2': nrm(ks[5], (DEPTH, D_FF, D_MODEL), D_FF ** -0.5),
        'mix_norm': gain(ks[6], (DEPTH, D_MODEL)),
        'ffn2_norm': gain(ks[7], (DEPTH, D_MODEL)),
        'ffn2_w1': nrm(ks[8], (DEPTH, D_MODEL, D_FF), d_in),
        'ffn2_w3': nrm(ks[9], (DEPTH, D_MODEL, D_FF), d_in),
        'ffn2_w2': nrm(ks[10], (DEPTH, D_FF, D_MODEL), D_FF ** -0.5),
        'ple_norm': gain(ks[11], (DEPTH, D_MODEL)),
        'ple_proj': nrm(ks[12], (DEPTH, PLE_DIM, D_MODEL), PLE_DIM ** -0.5),
        'ple_gate': nrm(ks[13], (DEPTH, D_MODEL, D_MODEL), d_in),
        's5_w_in': nrm(ks[14], (N_A, D_MODEL, D_MODEL), d_in),
        's5_a_re': -0.5 + 0.01 * jax.random.normal(ks[15], (N_A, S5_GROUPS, S5_STATE), f32),
        's5_a_im': a_im0 + 0.01 * jax.random.normal(ks[16], (N_A, S5_GROUPS, S5_STATE), f32),
        's5_log_dt': jax.random.uniform(ks[17], (N_A, S5_GROUPS), f32,
                                        math.log(DT_MIN), math.log(DT_MAX)),
        's5_b_re': nrm(ks[18], (N_A, S5_GROUPS, S5_STATE, S5_GROUP), (2 * S5_GROUP) ** -0.5),
        's5_b_im': nrm(ks[19], (N_A, S5_GROUPS, S5_STATE, S5_GROUP), (2 * S5_GROUP) ** -0.5),
        's5_c_re': nrm(ks[20], (N_A, S5_GROUPS, S5_GROUP, S5_STATE), S5_STATE ** -0.5),
        's5_c_im': nrm(ks[21], (N_A, S5_GROUPS, S5_GROUP, S5_STATE), S5_STATE ** -0.5),
        's5_d': jax.random.normal(ks[22], (N_A, D_MODEL), f32),
        's5_w_glu': nrm(ks[23], (N_A, D_MODEL, 2 * D_MODEL), d_in),
        'sb_w_qkv': nrm(ks[24], (N_B, D_MODEL, 3 * D_MODEL), d_in),
        'sb_w_o': nrm(ks[25], (N_B, D_MODEL, D_MODEL), d_in),
        'final_norm': gain(ks[26], (D_MODEL,)),
    }


def reference(x, p, ffn1_norm, ffn1_w1, ffn1_w3, ffn1_w2, mix_norm, ffn2_norm,
              ffn2_w1, ffn2_w3, ffn2_w2, ple_norm, ple_proj, ple_gate,
              s5_w_in, s5_a_re, s5_a_im, s5_log_dt, s5_b_re, s5_b_im,
              s5_c_re, s5_c_im, s5_d, s5_w_glu, sb_w_qkv, sb_w_o, final_norm):
    h = x
    for i in range(DEPTH):
        h = h + 0.5 * swiglu(rmsnorm(h, ffn1_norm[i]), ffn1_w1[i], ffn1_w3[i], ffn1_w2[i])
        hn = rmsnorm(h, mix_norm[i])
        j = i // N_MIXERS
        if i % N_MIXERS == 0:
            m = s5_mixer(hn, s5_w_in[j], s5_a_re[j], s5_a_im[j], s5_log_dt[j],
                         s5_b_re[j], s5_b_im[j], s5_c_re[j], s5_c_im[j],
                         s5_d[j], s5_w_glu[j])
        else:
            m = stick_breaking_mixer(hn, sb_w_qkv[j], sb_w_o[j])
        h = h + m
        h = h + 0.5 * swiglu(rmsnorm(h, ffn2_norm[i]), ffn2_w1[i], ffn2_w3[i], ffn2_w2[i])
        gate = jax.nn.sigmoid(rmsnorm(h, ple_norm[i]) @ ple_gate[i])
        h = h + (p[i].astype(h.dtype) @ ple_proj[i]) * gate
    return rmsnorm(h, final_norm)
```

```python
import functools

import jax
import jax.numpy as jnp
from jax import lax
from jax.experimental import pallas as pl
from jax.experimental.pallas import tpu as pltpu

F32 = jnp.float32
BF16 = jnp.bfloat16

EPS = 1e-6
S5_GROUP = 16
S5_STATE = 64
S5_CHUNK = 64
SB_HEAD_DIM = 64
LANES = 128
SB_TILE = 128
VMEM_LIMIT_BYTES = 56 * 1024 * 1024


def _params(*semantics):
    return pltpu.CompilerParams(dimension_semantics=semantics,
                                vmem_limit_bytes=VMEM_LIMIT_BYTES)


def _rms(x, g):
    return x * lax.rsqrt(jnp.mean(x * x, axis=-1, keepdims=True) + EPS) * g


def _dot(a, b):
    return jnp.dot(a, b, preferred_element_type=F32)


def _ffn_kernel(x_ref, g_ref, w1_ref, w3_ref, w2_ref, o_ref, xn_ref, acc_ref):
    j = pl.program_id(1)

    @pl.when(j == 0)
    def _():
        xn_ref[...] = _rms(x_ref[...], g_ref[...]).astype(BF16)
        acc_ref[...] = jnp.zeros_like(acc_ref)

    xn = xn_ref[...]
    a = jax.nn.silu(_dot(xn, w1_ref[...])) * _dot(xn, w3_ref[...])
    acc_ref[...] += _dot(a.astype(BF16), w2_ref[...])

    @pl.when(j == pl.num_programs(1) - 1)
    def _():
        o_ref[...] = x_ref[...] + 0.5 * acc_ref[...]


def _ffn(x, g, w1, w3, w2):
    n, d = x.shape
    dff = w1.shape[1]
    tm = min(1024, n)
    tf = 256
    return pl.pallas_call(
        _ffn_kernel,
        grid=(n // tm, dff // tf),
        in_specs=[
            pl.BlockSpec((tm, d), lambda i, j: (i, 0)),
            pl.BlockSpec((1, d), lambda i, j: (0, 0)),
            pl.BlockSpec((d, tf), lambda i, j: (0, j)),
            pl.BlockSpec((d, tf), lambda i, j: (0, j)),
            pl.BlockSpec((tf, d), lambda i, j: (j, 0)),
        ],
        out_specs=pl.BlockSpec((tm, d), lambda i, j: (i, 0)),
        out_shape=jax.ShapeDtypeStruct((n, d), F32),
        scratch_shapes=[pltpu.VMEM((tm, d), BF16), pltpu.VMEM((tm, d), F32)],
        compiler_params=_params("parallel", "arbitrary"),
        name="ffn",
    )(x, g.reshape(1, d), w1, w3, w2)


def _norm_mm_kernel(x_ref, g_ref, w_ref, o_ref, xn_ref):
    @pl.when(pl.program_id(1) == 0)
    def _():
        xn_ref[...] = _rms(x_ref[...], g_ref[...]).astype(BF16)

    o_ref[...] = _dot(xn_ref[...], w_ref[...]).astype(o_ref.dtype)


def _norm_mm(x, g, w):
    n, d = x.shape
    dout = w.shape[1]
    tm = min(1024, n)
    tn = min(1024, dout)
    return pl.pallas_call(
        _norm_mm_kernel,
        grid=(n // tm, dout // tn),
        in_specs=[
            pl.BlockSpec((tm, d), lambda i, j: (i, 0)),
            pl.BlockSpec((1, d), lambda i, j: (0, 0)),
            pl.BlockSpec((d, tn), lambda i, j: (0, j)),
        ],
        out_specs=pl.BlockSpec((tm, tn), lambda i, j: (i, j)),
        out_shape=jax.ShapeDtypeStruct((n, dout), BF16),
        scratch_shapes=[pltpu.VMEM((tm, d), BF16)],
        compiler_params=_params("parallel", "arbitrary"),
        name="norm_mm",
    )(x, g.reshape(1, d), w)


def _mm_res_kernel(h_ref, a_ref, w_ref, o_ref):
    o_ref[...] = h_ref[...] + _dot(a_ref[...], w_ref[...])


def _mm_res(h, a, w):
    n, d = h.shape
    tm = min(512, n)
    return pl.pallas_call(
        _mm_res_kernel,
        grid=(n // tm,),
        in_specs=[
            pl.BlockSpec((tm, d), lambda i: (i, 0)),
            pl.BlockSpec((tm, a.shape[1]), lambda i: (i, 0)),
            pl.BlockSpec(w.shape, lambda i: (0, 0)),
        ],
        out_specs=pl.BlockSpec((tm, d), lambda i: (i, 0)),
        out_shape=jax.ShapeDtypeStruct((n, d), F32),
        compiler_params=_params("parallel"),
        name="mm_res",
    )(h, a, w)


def _glu_kernel(h_ref, z_ref, w_ref, o_ref):
    d = h_ref.shape[1]
    r = _dot(z_ref[...], w_ref[...])
    o_ref[...] = h_ref[...] + r[:, :d] * jax.nn.sigmoid(r[:, d:])


def _glu(h, z, w):
    n, d = h.shape
    tm = min(512, n)
    return pl.pallas_call(
        _glu_kernel,
        grid=(n // tm,),
        in_specs=[
            pl.BlockSpec((tm, d), lambda i: (i, 0)),
            pl.BlockSpec((tm, d), lambda i: (i, 0)),
            pl.BlockSpec(w.shape, lambda i: (0, 0)),
        ],
        out_specs=pl.BlockSpec((tm, d), lambda i: (i, 0)),
        out_shape=jax.ShapeDtypeStruct((n, d), F32),
        compiler_params=_params("parallel"),
        name="glu",
    )(h, z, w)


def _ple_kernel(h_ref, p_ref, g_ref, wg_ref, wp_ref, fg_ref, o_ref, *, final):
    h = h_ref[...]
    gate = jax.nn.sigmoid(_dot(_rms(h, g_ref[...]).astype(BF16), wg_ref[...]))
    out = h + _dot(p_ref[...].astype(BF16), wp_ref[...]) * gate
    if final:
        out = _rms(out, fg_ref[...])
    o_ref[...] = out


def _ple(h, p, g, w_gate, w_proj, final_g, final):
    n, d = h.shape
    tm = min(512, n)
    return pl.pallas_call(
        functools.partial(_ple_kernel, final=final),
        grid=(n // tm,),
        in_specs=[
            pl.BlockSpec((tm, d), lambda i: (i, 0)),
            pl.BlockSpec((tm, p.shape[1]), lambda i: (i, 0)),
            pl.BlockSpec((1, d), lambda i: (0, 0)),
            pl.BlockSpec(w_gate.shape, lambda i: (0, 0)),
            pl.BlockSpec(w_proj.shape, lambda i: (0, 0)),
            pl.BlockSpec((1, d), lambda i: (0, 0)),
        ],
        out_specs=pl.BlockSpec((tm, d), lambda i: (i, 0)),
        out_shape=jax.ShapeDtypeStruct((n, d), F32),
        compiler_params=_params("parallel"),
        name="ple",
    )(h, p, g.reshape(1, d), w_gate, w_proj, final_g.reshape(1, d))


def _s5_kernel(x_ref, k_ref, b_ref, c_ref, dec_ref, d_ref, z_ref, m_ref, sp_ref, *, nb, nc):
    hh = k_ref.shape[1]
    t = m_ref.shape[0] // hh
    ns = sp_ref.shape[1] // 2

    taps = k_ref[0]
    lane = lax.broadcasted_iota(jnp.int32, taps.shape, 1)
    m_ref[0:hh, :] = taps.astype(BF16)
    for s in range(1, t):
        shifted = jnp.where(lane >= s * hh, pltpu.roll(taps, s * hh, axis=1), 0.0)
        m_ref[s * hh:(s + 1) * hh, :] = shifted.astype(BF16)

    x = x_ref[0]
    s_in = _dot(x, b_ref[0])
    s_in_sw = pltpu.roll(s_in, ns, axis=1)
    dec = dec_ref[0]
    a_rr, a_s, a_sw = dec[0:1], dec[1:2], dec[2:3]
    st = jnp.zeros((nb, 2 * ns), F32)
    sw = jnp.zeros((nb, 2 * ns), F32)
    for c in range(nc):
        sp_ref[c * nb:(c + 1) * nb, :] = st
        rows = slice(c * nb, (c + 1) * nb)
        st, sw = (st * a_rr + sw * a_s + s_in[rows],
                  sw * a_rr + st * a_sw + s_in_sw[rows])

    y = _dot(x, m_ref[...]) + _dot(sp_ref[...].astype(BF16), c_ref[0])
    y = y + d_ref[0] * x.astype(F32)
    z_ref[0] = jax.nn.gelu(y).astype(BF16)


def _s5_weights(a_re, a_im, log_dt, b_re, b_im, c_re, c_im, d_skip):
    t = S5_CHUNK
    g, ns, hh = b_re.shape
    hp = lax.Precision.HIGHEST
    lam_re = jnp.minimum(a_re.astype(F32), -1e-4)
    lam_im = a_im.astype(F32)
    dt = jnp.exp(log_dt.astype(F32))[:, None]
    mag = jnp.exp(lam_re * dt)
    abar_re = mag * jnp.cos(lam_im * dt)
    abar_im = mag * jnp.sin(lam_im * dt)
    den = lam_re * lam_re + lam_im * lam_im
    nr = abar_re - 1.0
    ni = abar_im
    fr = (nr * lam_re + ni * lam_im) / den
    fi = (ni * lam_re - nr * lam_im) / den
    bre = b_re.astype(F32)
    bim = b_im.astype(F32)
    bbar_re = fr[..., None] * bre - fi[..., None] * bim
    bbar_im = fr[..., None] * bim + fi[..., None] * bre
    cre = c_re.astype(F32)
    cim = c_im.astype(F32)

    k = jnp.arange(t + 1, dtype=F32)[:, None, None]
    pmag = jnp.exp(k * (lam_re * dt))
    pw_re = pmag * jnp.cos(k * (lam_im * dt))
    pw_im = pmag * jnp.sin(k * (lam_im * dt))

    cw_re = cre[None] * pw_re[:t, :, None, :] - cim[None] * pw_im[:t, :, None, :]
    cw_im = cre[None] * pw_im[:t, :, None, :] + cim[None] * pw_re[:t, :, None, :]
    taps = (jnp.einsum('kgop,gpi->gkoi', cw_re, bbar_re, precision=hp)
            - jnp.einsum('kgop,gpi->gkoi', cw_im, bbar_im, precision=hp))
    taps = taps.transpose(0, 3, 1, 2).reshape(g, hh, t * hh)

    rev_re = pw_re[:t][::-1]
    rev_im = pw_im[:t][::-1]
    bm_re = rev_re[:, :, :, None] * bbar_re[None] - rev_im[:, :, :, None] * bbar_im[None]
    bm_im = rev_re[:, :, :, None] * bbar_im[None] + rev_im[:, :, :, None] * bbar_re[None]
    bmat = jnp.concatenate([bm_re, bm_im], axis=2)
    bmat = bmat.transpose(1, 0, 3, 2).reshape(g, t * hh, 2 * ns)

    nx_re = pw_re[1:]
    nx_im = pw_im[1:]
    cm_re = cre[None] * nx_re[:, :, None, :] - cim[None] * nx_im[:, :, None, :]
    cm_im = cre[None] * nx_im[:, :, None, :] + cim[None] * nx_re[:, :, None, :]
    cmat = jnp.concatenate([cm_re, -cm_im], axis=3)
    cmat = cmat.transpose(1, 3, 0, 2).reshape(g, 2 * ns, t * hh)

    at_re, at_im = pw_re[t], pw_im[t]
    dec = jnp.stack([
        jnp.concatenate([at_re, at_re], axis=1),
        jnp.concatenate([-at_im, at_im], axis=1),
        jnp.concatenate([at_im, -at_im], axis=1),
    ], axis=1)
    dec = jnp.pad(dec, ((0, 0), (0, 5), (0, 0)))

    dflat = jnp.tile(d_skip.astype(F32).reshape(g, 1, hh), (1, t, 1)).reshape(g, 1, t * hh)
    return taps, bmat.astype(BF16), cmat.astype(BF16), dec, dflat


def _s5_core(u, weights, b, l):
    taps, bmat, cmat, dec, dflat = weights
    t = S5_CHUNK
    g, hh = taps.shape[0], taps.shape[1]
    ns2 = bmat.shape[2]
    nc = l // t
    rows = nc * b
    th = t * hh
    x = u.reshape(b, nc, t, g, hh).transpose(3, 1, 0, 2, 4).reshape(g, rows, th)
    z = pl.pallas_call(
        functools.partial(_s5_kernel, nb=b, nc=nc),
        grid=(g,),
        in_specs=[
            pl.BlockSpec((1, rows, th), lambda i: (i, 0, 0)),
            pl.BlockSpec((1, hh, th), lambda i: (i, 0, 0)),
            pl.BlockSpec((1, th, ns2), lambda i: (i, 0, 0)),
            pl.BlockSpec((1, ns2, th), lambda i: (i, 0, 0)),
            pl.BlockSpec((1, 8, ns2), lambda i: (i, 0, 0)),
            pl.BlockSpec((1, 1, th), lambda i: (i, 0, 0)),
        ],
        out_specs=pl.BlockSpec((1, rows, th), lambda i: (i, 0, 0)),
        out_shape=jax.ShapeDtypeStruct((g, rows, th), BF16),
        scratch_shapes=[pltpu.VMEM((th, th), BF16), pltpu.VMEM((rows, ns2), F32)],
        compiler_params=_params("parallel"),
        name="s5_core",
    )(x, taps, bmat, cmat, dec, dflat)
    return z.reshape(g, nc, b, t, hh).transpose(2, 1, 3, 0, 4).reshape(b * l, g * hh)


def _sb_kernel(q_ref, k_ref, v_ref, o_ref, carry_ref, acc_ref):
    tq = SB_TILE
    nq = q_ref.shape[1] // tq
    lane = lax.broadcasted_iota(jnp.int32, (tq, LANES), 1)
    row = lax.broadcasted_iota(jnp.int32, (tq, LANES), 0)
    head0 = lane < SB_HEAD_DIM
    strict = lane < row
    trow = lax.broadcasted_iota(jnp.int32, (tq, 2 * LANES), 0)
    tcol = lax.broadcasted_iota(jnp.int32, (tq, 2 * LANES), 1)
    tri = jnp.where((tcol >= LANES) | (trow > tcol), -1.0, 0.0).astype(BF16)

    def tile(qh, kb, vb, diag):
        for h in range(2):
            z = lax.dot_general(qh[h], kb, (((1,), (1,)), ((), ())), preferred_element_type=F32)
            sp = jnp.log(1.0 + jnp.exp(-jnp.abs(z)))
            nlk = jnp.maximum(z, 0.0) + sp
            ls = z - nlk
            if diag:
                nlk = jnp.where(strict, nlk, 0.0)
            hi = nlk.astype(BF16)
            lo = (nlk - hi.astype(F32)).astype(BF16)
            r = _dot(hi, tri) + _dot(lo, tri)
            carry = carry_ref[h]
            att = jnp.exp(ls + r[:, :LANES] + carry)
            if diag:
                att = jnp.where(strict, att, 0.0)
            acc_ref[h] += _dot(att.astype(BF16), vb)
            carry_ref[h] = carry + r[:, LANES:]

    def q_tile(i, _):
        q0 = pl.multiple_of(i * tq, tq)
        q = q_ref[0, pl.ds(q0, tq), :]
        zero = jnp.zeros_like(q)
        qh = (jnp.where(head0, q, zero), jnp.where(head0, zero, q))
        carry_ref[...] = jnp.zeros_like(carry_ref)
        acc_ref[...] = jnp.zeros_like(acc_ref)
        tile(qh, k_ref[0, pl.ds(q0, tq), :], v_ref[0, pl.ds(q0, tq), :], True)

        def k_tile(jj, _):
            k0 = pl.multiple_of((i - 1 - jj) * tq, tq)
            tile(qh, k_ref[0, pl.ds(k0, tq), :], v_ref[0, pl.ds(k0, tq), :], False)
            return 0

        lax.fori_loop(0, i, k_tile, 0)
        o_ref[0, pl.ds(q0, tq), :] = jnp.where(head0, acc_ref[0], acc_ref[1]).astype(o_ref.dtype)
        return 0

    lax.fori_loop(0, nq, q_tile, 0)


def _sb_attention(qkv, d):
    b, l, _ = qkv.shape
    nhp = d // LANES
    return pl.pallas_call(
        _sb_kernel,
        grid=(b, nhp),
        in_specs=[
            pl.BlockSpec((1, l, LANES), lambda i, j: (i, 0, j)),
            pl.BlockSpec((1, l, LANES), lambda i, j: (i, 0, nhp + j)),
            pl.BlockSpec((1, l, LANES), lambda i, j: (i, 0, 2 * nhp + j)),
        ],
        out_specs=pl.BlockSpec((1, l, LANES), lambda i, j: (i, 0, j)),
        out_shape=jax.ShapeDtypeStruct((b, l, d), BF16),
        scratch_shapes=[pltpu.VMEM((2, SB_TILE, LANES), F32), pltpu.VMEM((2, SB_TILE, LANES), F32)],
        compiler_params=_params("parallel", "parallel"),
        name="sb_attention",
    )(qkv, qkv, qkv)


def kernel(x, p, ffn1_norm, ffn1_w1, ffn1_w3, ffn1_w2, mix_norm, ffn2_norm, ffn2_w1, ffn2_w3, ffn2_w2, ple_norm, ple_proj, ple_gate, s5_w_in, s5_a_re, s5_a_im, s5_log_dt, s5_b_re, s5_b_im, s5_c_re, s5_c_im, s5_d, s5_w_glu, sb_w_qkv, sb_w_o, final_norm):
    b, l, d = x.shape
    depth = p.shape[0]
    n = b * l
    bf = lambda w: w.astype(BF16)
    h = x.reshape(n, d)
    for i in range(depth):
        h = _ffn(h, ffn1_norm[i], bf(ffn1_w1[i]), bf(ffn1_w3[i]), bf(ffn1_w2[i]))
        j = i // 2
        if i % 2 == 0:
            u = _norm_mm(h, mix_norm[i], bf(s5_w_in[j]))
            weights = _s5_weights(s5_a_re[j], s5_a_im[j], s5_log_dt[j], s5_b_re[j], s5_b_im[j],
                                  s5_c_re[j], s5_c_im[j], s5_d[j])
            h = _glu(h, _s5_core(u, weights, b, l), bf(s5_w_glu[j]))
        else:
            col_scale = jnp.where(jnp.arange(3 * d) < d, SB_HEAD_DIM ** -0.5, 1.0).astype(F32)
            qkv = _norm_mm(h, mix_norm[i], bf(sb_w_qkv[j] * col_scale))
            o = _sb_attention(qkv.reshape(b, l, 3 * d), d)
            h = _mm_res(h, o.reshape(n, d), bf(sb_w_o[j]))
        h = _ffn(h, ffn2_norm[i], bf(ffn2_w1[i]), bf(ffn2_w3[i]), bf(ffn2_w2[i]))
        h = _ple(h, p[i].reshape(n, -1), ple_norm[i], bf(ple_gate[i]), bf(ple_proj[i]),
                 final_norm, final=(i == depth - 1))
    return h.reshape(b, l, d)
```

```python
import functools

import jax
import jax.numpy as jnp
from jax import lax
from jax.experimental import pallas as pl
from jax.experimental.pallas import tpu as pltpu

F32 = jnp.float32
BF16 = jnp.bfloat16

EPS = 1e-6
S5_GROUP = 16
S5_STATE = 64
S5_CHUNK = 64
SB_HEAD_DIM = 64
LANES = 128
SB_TQ = 512
SB_TK = 128
VMEM_LIMIT_BYTES = 56 * 1024 * 1024


def _params(*semantics):
    return pltpu.CompilerParams(dimension_semantics=semantics,
                                vmem_limit_bytes=VMEM_LIMIT_BYTES)


def _rms(x, g):
    return x * lax.rsqrt(jnp.mean(x * x, axis=-1, keepdims=True) + EPS) * g


def _dot(a, b):
    return jnp.dot(a, b, preferred_element_type=F32)


def _ffn_kernel(x_ref, g_ref, w1_ref, w3_ref, w2_ref, o_ref, xn_ref, acc_ref):
    j = pl.program_id(1)

    @pl.when(j == 0)
    def _():
        xn_ref[...] = _rms(x_ref[...], g_ref[...]).astype(BF16)
        acc_ref[...] = jnp.zeros_like(acc_ref)

    xn = xn_ref[...]
    a = jax.nn.silu(_dot(xn, w1_ref[...])) * _dot(xn, w3_ref[...])
    acc_ref[...] += _dot(a.astype(BF16), w2_ref[...])

    @pl.when(j == pl.num_programs(1) - 1)
    def _():
        o_ref[...] = x_ref[...] + 0.5 * acc_ref[...]


def _ffn(x, g, w1, w3, w2):
    n, d = x.shape
    dff = w1.shape[1]
    tm = min(1024, n)
    tf = 256
    return pl.pallas_call(
        _ffn_kernel,
        grid=(n // tm, dff // tf),
        in_specs=[
            pl.BlockSpec((tm, d), lambda i, j: (i, 0)),
            pl.BlockSpec((1, d), lambda i, j: (0, 0)),
            pl.BlockSpec((d, tf), lambda i, j: (0, j)),
            pl.BlockSpec((d, tf), lambda i, j: (0, j)),
            pl.BlockSpec((tf, d), lambda i, j: (j, 0)),
        ],
        out_specs=pl.BlockSpec((tm, d), lambda i, j: (i, 0)),
        out_shape=jax.ShapeDtypeStruct((n, d), F32),
        scratch_shapes=[pltpu.VMEM((tm, d), BF16), pltpu.VMEM((tm, d), F32)],
        compiler_params=_params("parallel", "arbitrary"),
        name="ffn",
    )(x, g.reshape(1, d), w1, w3, w2)


def _norm_mm_kernel(x_ref, g_ref, w_ref, o_ref, xn_ref):
    @pl.when(pl.program_id(1) == 0)
    def _():
        xn_ref[...] = _rms(x_ref[...], g_ref[...]).astype(BF16)

    o_ref[...] = _dot(xn_ref[...], w_ref[...]).astype(o_ref.dtype)


def _norm_mm(x, g, w):
    n, d = x.shape
    dout = w.shape[1]
    tm = min(1024, n)
    tn = min(1024, dout)
    return pl.pallas_call(
        _norm_mm_kernel,
        grid=(n // tm, dout // tn),
        in_specs=[
            pl.BlockSpec((tm, d), lambda i, j: (i, 0)),
            pl.BlockSpec((1, d), lambda i, j: (0, 0)),
            pl.BlockSpec((d, tn), lambda i, j: (0, j)),
        ],
        out_specs=pl.BlockSpec((tm, tn), lambda i, j: (i, j)),
        out_shape=jax.ShapeDtypeStruct((n, dout), BF16),
        scratch_shapes=[pltpu.VMEM((tm, d), BF16)],
        compiler_params=_params("parallel", "arbitrary"),
        name="norm_mm",
    )(x, g.reshape(1, d), w)


def _mm_res_kernel(h_ref, a_ref, w_ref, o_ref):
    o_ref[...] = h_ref[...] + _dot(a_ref[...], w_ref[...])


def _mm_res(h, a, w):
    n, d = h.shape
    tm = min(512, n)
    return pl.pallas_call(
        _mm_res_kernel,
        grid=(n // tm,),
        in_specs=[
            pl.BlockSpec((tm, d), lambda i: (i, 0)),
            pl.BlockSpec((tm, a.shape[1]), lambda i: (i, 0)),
            pl.BlockSpec(w.shape, lambda i: (0, 0)),
        ],
        out_specs=pl.BlockSpec((tm, d), lambda i: (i, 0)),
        out_shape=jax.ShapeDtypeStruct((n, d), F32),
        compiler_params=_params("parallel"),
        name="mm_res",
    )(h, a, w)


def _glu_kernel(h_ref, z_ref, w_ref, o_ref):
    d = h_ref.shape[1]
    r = _dot(z_ref[...], w_ref[...])
    o_ref[...] = h_ref[...] + r[:, :d] * jax.nn.sigmoid(r[:, d:])


def _glu(h, z, w):
    n, d = h.shape
    tm = min(512, n)
    return pl.pallas_call(
        _glu_kernel,
        grid=(n // tm,),
        in_specs=[
            pl.BlockSpec((tm, d), lambda i: (i, 0)),
            pl.BlockSpec((tm, d), lambda i: (i, 0)),
            pl.BlockSpec(w.shape, lambda i: (0, 0)),
        ],
        out_specs=pl.BlockSpec((tm, d), lambda i: (i, 0)),
        out_shape=jax.ShapeDtypeStruct((n, d), F32),
        compiler_params=_params("parallel"),
        name="glu",
    )(h, z, w)


def _ple_kernel(h_ref, p_ref, g_ref, wg_ref, wp_ref, fg_ref, o_ref, *, final):
    h = h_ref[...]
    gate = jax.nn.sigmoid(_dot(_rms(h, g_ref[...]).astype(BF16), wg_ref[...]))
    out = h + _dot(p_ref[...].astype(BF16), wp_ref[...]) * gate
    if final:
        out = _rms(out, fg_ref[...])
    o_ref[...] = out


def _ple(h, p, g, w_gate, w_proj, final_g, final):
    n, d = h.shape
    tm = min(512, n)
    return pl.pallas_call(
        functools.partial(_ple_kernel, final=final),
        grid=(n // tm,),
        in_specs=[
            pl.BlockSpec((tm, d), lambda i: (i, 0)),
            pl.BlockSpec((tm, p.shape[1]), lambda i: (i, 0)),
            pl.BlockSpec((1, d), lambda i: (0, 0)),
            pl.BlockSpec(w_gate.shape, lambda i: (0, 0)),
            pl.BlockSpec(w_proj.shape, lambda i: (0, 0)),
            pl.BlockSpec((1, d), lambda i: (0, 0)),
        ],
        out_specs=pl.BlockSpec((tm, d), lambda i: (i, 0)),
        out_shape=jax.ShapeDtypeStruct((n, d), F32),
        compiler_params=_params("parallel"),
        name="ple",
    )(h, p, g.reshape(1, d), w_gate, w_proj, final_g.reshape(1, d))


def _s5_kernel(x_ref, k_ref, b_ref, c_ref, dec_ref, d_ref, z_ref, m_ref, sp_ref, *, nb, nc):
    hh = k_ref.shape[1]
    t = m_ref.shape[0] // hh
    ns = sp_ref.shape[1] // 2

    taps = k_ref[0]
    lane = lax.broadcasted_iota(jnp.int32, taps.shape, 1)
    m_ref[0:hh, :] = taps.astype(BF16)
    for s in range(1, t):
        shifted = jnp.where(lane >= s * hh, pltpu.roll(taps, s * hh, axis=1), 0.0)
        m_ref[s * hh:(s + 1) * hh, :] = shifted.astype(BF16)

    x = x_ref[0]
    s_in = _dot(x, b_ref[0])
    s_in_sw = pltpu.roll(s_in, ns, axis=1)
    dec = dec_ref[0]
    a_rr, a_s, a_sw = dec[0:1], dec[1:2], dec[2:3]
    st = jnp.zeros((nb, 2 * ns), F32)
    sw = jnp.zeros((nb, 2 * ns), F32)
    for c in range(nc):
        sp_ref[c * nb:(c + 1) * nb, :] = st
        rows = slice(c * nb, (c + 1) * nb)
        st, sw = (st * a_rr + sw * a_s + s_in[rows],
                  sw * a_rr + st * a_sw + s_in_sw[rows])

    y = _dot(x, m_ref[...]) + _dot(sp_ref[...].astype(BF16), c_ref[0])
    y = y + d_ref[0] * x.astype(F32)
    z_ref[0] = jax.nn.gelu(y).astype(BF16)


def _s5_weights(a_re, a_im, log_dt, b_re, b_im, c_re, c_im, d_skip):
    t = S5_CHUNK
    g, ns, hh = b_re.shape
    hp = lax.Precision.HIGHEST
    lam_re = jnp.minimum(a_re.astype(F32), -1e-4)
    lam_im = a_im.astype(F32)
    dt = jnp.exp(log_dt.astype(F32))[:, None]
    mag = jnp.exp(lam_re * dt)
    abar_re = mag * jnp.cos(lam_im * dt)
    abar_im = mag * jnp.sin(lam_im * dt)
    den = lam_re * lam_re + lam_im * lam_im
    nr = abar_re - 1.0
    ni = abar_im
    fr = (nr * lam_re + ni * lam_im) / den
    fi = (ni * lam_re - nr * lam_im) / den
    bre = b_re.astype(F32)
    bim = b_im.astype(F32)
    bbar_re = fr[..., None] * bre - fi[..., None] * bim
    bbar_im = fr[..., None] * bim + fi[..., None] * bre
    cre = c_re.astype(F32)
    cim = c_im.astype(F32)

    k = jnp.arange(t + 1, dtype=F32)[:, None, None]
    pmag = jnp.exp(k * (lam_re * dt))
    pw_re = pmag * jnp.cos(k * (lam_im * dt))
    pw_im = pmag * jnp.sin(k * (lam_im * dt))

    cw_re = cre[None] * pw_re[:t, :, None, :] - cim[None] * pw_im[:t, :, None, :]
    cw_im = cre[None] * pw_im[:t, :, None, :] + cim[None] * pw_re[:t, :, None, :]
    taps = (jnp.einsum('kgop,gpi->gkoi', cw_re, bbar_re, precision=hp)
            - jnp.einsum('kgop,gpi->gkoi', cw_im, bbar_im, precision=hp))
    taps = taps.transpose(0, 3, 1, 2).reshape(g, hh, t * hh)

    rev_re = pw_re[:t][::-1]
    rev_im = pw_im[:t][::-1]
    bm_re = rev_re[:, :, :, None] * bbar_re[None] - rev_im[:, :, :, None] * bbar_im[None]
    bm_im = rev_re[:, :, :, None] * bbar_im[None] + rev_im[:, :, :, None] * bbar_re[None]
    bmat = jnp.concatenate([bm_re, bm_im], axis=2)
    bmat = bmat.transpose(1, 0, 3, 2).reshape(g, t * hh, 2 * ns)

    nx_re = pw_re[1:]
    nx_im = pw_im[1:]
    cm_re = cre[None] * nx_re[:, :, None, :] - cim[None] * nx_im[:, :, None, :]
    cm_im = cre[None] * nx_im[:, :, None, :] + cim[None] * nx_re[:, :, None, :]
    cmat = jnp.concatenate([cm_re, -cm_im], axis=3)
    cmat = cmat.transpose(1, 3, 0, 2).reshape(g, 2 * ns, t * hh)

    at_re, at_im = pw_re[t], pw_im[t]
    dec = jnp.stack([
        jnp.concatenate([at_re, at_re], axis=1),
        jnp.concatenate([-at_im, at_im], axis=1),
        jnp.concatenate([at_im, -at_im], axis=1),
    ], axis=1)
    dec = jnp.pad(dec, ((0, 0), (0, 5), (0, 0)))

    dflat = jnp.tile(d_skip.astype(F32).reshape(g, 1, hh), (1, t, 1)).reshape(g, 1, t * hh)
    return taps, bmat.astype(BF16), cmat.astype(BF16), dec, dflat


def _s5_core(u, weights, b, l):
    taps, bmat, cmat, dec, dflat = weights
    t = S5_CHUNK
    g, hh = taps.shape[0], taps.shape[1]
    ns2 = bmat.shape[2]
    nc = l // t
    rows = nc * b
    th = t * hh
    x = u.reshape(b, nc, t, g, hh).transpose(3, 1, 0, 2, 4).reshape(g, rows, th)
    z = pl.pallas_call(
        functools.partial(_s5_kernel, nb=b, nc=nc),
        grid=(g,),
        in_specs=[
            pl.BlockSpec((1, rows, th), lambda i: (i, 0, 0)),
            pl.BlockSpec((1, hh, th), lambda i: (i, 0, 0)),
            pl.BlockSpec((1, th, ns2), lambda i: (i, 0, 0)),
            pl.BlockSpec((1, ns2, th), lambda i: (i, 0, 0)),
            pl.BlockSpec((1, 8, ns2), lambda i: (i, 0, 0)),
            pl.BlockSpec((1, 1, th), lambda i: (i, 0, 0)),
        ],
        out_specs=pl.BlockSpec((1, rows, th), lambda i: (i, 0, 0)),
        out_shape=jax.ShapeDtypeStruct((g, rows, th), BF16),
        scratch_shapes=[pltpu.VMEM((th, th), BF16), pltpu.VMEM((rows, ns2), F32)],
        compiler_params=_params("parallel"),
        name="s5_core",
    )(x, taps, bmat, cmat, dec, dflat)
    return z.reshape(g, nc, b, t, hh).transpose(2, 1, 3, 0, 4).reshape(b * l, g * hh)


def _sb_kernel(q_ref, k_ref, v_ref, o_ref, carry_ref, acc_ref):
    tk = SB_TK
    tq = carry_ref.shape[1]
    nsub = tq // tk
    nq = q_ref.shape[1] // tq
    two = 2 * LANES

    lane = lax.broadcasted_iota(jnp.int32, (tk, two), 1)
    row = lax.broadcasted_iota(jnp.int32, (tk, two), 0)
    strict = (lane & (LANES - 1)) < row
    trow = lax.broadcasted_iota(jnp.int32, (two, two), 0)
    tcol = lax.broadcasted_iota(jnp.int32, (two, two), 1)
    tri = jnp.where((tcol >= LANES) | ((trow & (LANES - 1)) > tcol), -1.0, 0.0).astype(BF16)
    head0 = lax.broadcasted_iota(jnp.int32, (tk, LANES), 1) < SB_HEAD_DIM

    def split_heads(t):
        zero = jnp.zeros_like(t)
        return jnp.concatenate([jnp.where(head0, t, zero), jnp.where(head0, zero, t)], axis=0)

    def mask_diag(a):
        top = jnp.where(strict, a[:tk], 0.0)
        return top if a.shape[0] == tk else jnp.concatenate([top, a[tk:]], axis=0)

    def scores(q_rows, kb, diag):
        z = lax.dot_general(q_rows, split_heads(kb), (((1,), (1,)), ((), ())),
                            preferred_element_type=F32)
        nlk = jnp.maximum(z, 0.0) + jnp.log(1.0 + jnp.exp(-jnp.abs(z)))
        ls = z - nlk
        if diag:
            nlk = mask_diag(nlk)
        hi = nlk.astype(BF16)
        lo = (nlk - hi.astype(F32)).astype(BF16)
        r = [_dot(jnp.concatenate([hi[:, h * LANES:(h + 1) * LANES],
                                   lo[:, h * LANES:(h + 1) * LANES]], axis=1), tri)
             for h in range(2)]
        return ls, r

    def attend(ls, r, carry, diag):
        arg = jnp.concatenate([r[h][:, :LANES] + carry[h] for h in range(2)], axis=1) + ls
        att = jnp.exp(arg)
        if diag:
            att = mask_diag(att)
        return att.astype(BF16), [carry[h] + r[h][:, LANES:] for h in range(2)]

    def q_tile(i, _):
        q0 = pl.multiple_of(i * tq, tq)
        q = q_ref[0, pl.ds(q0, tq), :]
        carry_ref[...] = jnp.zeros_like(carry_ref)
        acc_ref[...] = jnp.zeros_like(acc_ref)

        for kk in reversed(range(nsub)):
            r0 = kk * tk
            k0 = pl.multiple_of(q0 + r0, tk)
            ls, r = scores(q[r0:], k_ref[0, pl.ds(k0, tk), :], True)
            att, carry = attend(ls, r, [carry_ref[h, r0:, :] for h in range(2)], True)
            for h in range(2):
                carry_ref[h, r0:, :] = carry[h]
            acc_ref[r0:, :] += _dot(att, split_heads(v_ref[0, pl.ds(k0, tk), :]))

        def k_pair(jj, _):
            ka = pl.multiple_of(q0 - (2 * jj + 1) * tk, tk)
            kb = pl.multiple_of(q0 - (2 * jj + 2) * tk, tk)
            ls_a, r_a = scores(q, k_ref[0, pl.ds(ka, tk), :], False)
            ls_b, r_b = scores(q, k_ref[0, pl.ds(kb, tk), :], False)
            carry = [carry_ref[h] for h in range(2)]
            att_a, carry = attend(ls_a, r_a, carry, False)
            att_b, carry = attend(ls_b, r_b, carry, False)
            for h in range(2):
                carry_ref[h] = carry[h]
            vv = jnp.concatenate([split_heads(v_ref[0, pl.ds(ka, tk), :]),
                                  split_heads(v_ref[0, pl.ds(kb, tk), :])], axis=0)
            acc_ref[...] += _dot(jnp.concatenate([att_a, att_b], axis=1), vv)
            return 0

        lax.fori_loop(0, i * (nsub // 2), k_pair, 0)
        o_ref[0, pl.ds(q0, tq), :] = acc_ref[...].astype(o_ref.dtype)
        return 0

    lax.fori_loop(0, nq, q_tile, 0)


def _sb_attention(qkv, d):
    b, l, _ = qkv.shape
    nhp = d // LANES
    tq = min(SB_TQ, l)
    assert tq % (2 * SB_TK) == 0 and l % tq == 0
    return pl.pallas_call(
        _sb_kernel,
        grid=(b, nhp),
        in_specs=[
            pl.BlockSpec((1, l, LANES), lambda i, j: (i, 0, j)),
            pl.BlockSpec((1, l, LANES), lambda i, j: (i, 0, nhp + j)),
            pl.BlockSpec((1, l, LANES), lambda i, j: (i, 0, 2 * nhp + j)),
        ],
        out_specs=pl.BlockSpec((1, l, LANES), lambda i, j: (i, 0, j)),
        out_shape=jax.ShapeDtypeStruct((b, l, d), BF16),
        scratch_shapes=[pltpu.VMEM((2, tq, LANES), F32), pltpu.VMEM((tq, LANES), F32)],
        compiler_params=_params("parallel", "parallel"),
        name="sb_attention",
    )(qkv, qkv, qkv)


def kernel(x, p, ffn1_norm, ffn1_w1, ffn1_w3, ffn1_w2, mix_norm, ffn2_norm, ffn2_w1, ffn2_w3, ffn2_w2, ple_norm, ple_proj, ple_gate, s5_w_in, s5_a_re, s5_a_im, s5_log_dt, s5_b_re, s5_b_im, s5_c_re, s5_c_im, s5_d, s5_w_glu, sb_w_qkv, sb_w_o, final_norm):
    b, l, d = x.shape
    depth = p.shape[0]
    n = b * l
    bf = lambda w: w.astype(BF16)
    h = x.reshape(n, d)
    for i in range(depth):
        h = _ffn(h, ffn1_norm[i], bf(ffn1_w1[i]), bf(ffn1_w3[i]), bf(ffn1_w2[i]))
        j = i // 2
        if i % 2 == 0:
            u = _norm_mm(h, mix_norm[i], bf(s5_w_in[j]))
            weights = _s5_weights(s5_a_re[j], s5_a_im[j], s5_log_dt[j], s5_b_re[j], s5_b_im[j],
                                  s5_c_re[j], s5_c_im[j], s5_d[j])
            h = _glu(h, _s5_core(u, weights, b, l), bf(s5_w_glu[j]))
        else:
            col_scale = jnp.where(jnp.arange(3 * d) < d, SB_HEAD_DIM ** -0.5, 1.0).astype(F32)
            qkv = _norm_mm(h, mix_norm[i], bf(sb_w_qkv[j] * col_scale))
            o = _sb_attention(qkv.reshape(b, l, 3 * d), d)
            h = _mm_res(h, o.reshape(n, d), bf(sb_w_o[j]))
        h = _ffn(h, ffn2_norm[i], bf(ffn2_w1[i]), bf(ffn2_w3[i]), bf(ffn2_w2[i]))
        h = _ple(h, p[i].reshape(n, -1), ple_norm[i], bf(ple_gate[i]), bf(ple_proj[i]),
                 final_norm, final=(i == depth - 1))
    return h.reshape(b, l, d)
```

```python
import functools

import jax
import jax.numpy as jnp
from jax import lax
from jax.experimental import pallas as pl
from jax.experimental.pallas import tpu as pltpu

F32 = jnp.float32
BF16 = jnp.bfloat16

EPS = 1e-6
S5_GROUP = 16
S5_STATE = 64
S5_CHUNK = 64
SB_HEAD_DIM = 64
LANES = 128
SB_TQ = 512
SB_TK = 128
SB_LOG_ZERO = -104.0
FFN_CHUNK = 256
MM_CHUNK = 512
VMEM_LIMIT_BYTES = 56 * 1024 * 1024


def _params(*semantics):
    return pltpu.CompilerParams(dimension_semantics=semantics,
                                vmem_limit_bytes=VMEM_LIMIT_BYTES)


def _rms(x, g):
    return x * lax.rsqrt(jnp.mean(x * x, axis=-1, keepdims=True) + EPS) * g


def _dot(a, b):
    return jnp.dot(a, b, preferred_element_type=F32)


def _ffn_kernel(x_ref, g_ref, w1_ref, w3_ref, w2_ref, o_ref):
    x = x_ref[...]
    xn = _rms(x, g_ref[...]).astype(BF16)
    o_ref[...] = x
    dff = w1_ref.shape[1]
    for c in range(0, dff, FFN_CHUNK):
        a = jax.nn.silu(_dot(xn, w1_ref[:, c:c + FFN_CHUNK])) * _dot(xn, w3_ref[:, c:c + FFN_CHUNK])
        o_ref[...] += _dot((0.5 * a).astype(BF16), w2_ref[c:c + FFN_CHUNK, :])


def _resident(shape):
    return pl.BlockSpec(shape, lambda *_: (0,) * len(shape), pipeline_mode=pl.Buffered(1))


def _ffn(x, g, w1, w3, w2):
    n, d = x.shape
    dff = w1.shape[1]
    tm = min(512, n)
    assert dff % FFN_CHUNK == 0
    return pl.pallas_call(
        _ffn_kernel,
        grid=(n // tm,),
        in_specs=[
            pl.BlockSpec((tm, d), lambda i: (i, 0)),
            _resident((1, d)),
            _resident((d, dff)),
            _resident((d, dff)),
            _resident((dff, d)),
        ],
        out_specs=pl.BlockSpec((tm, d), lambda i: (i, 0)),
        out_shape=jax.ShapeDtypeStruct((n, d), F32),
        compiler_params=_params("parallel"),
        name="ffn",
    )(x, g.reshape(1, d), w1, w3, w2)


def _norm_mm_kernel(x_ref, g_ref, w_ref, o_ref):
    xn = _rms(x_ref[...], g_ref[...]).astype(BF16)
    for c in range(0, w_ref.shape[1], MM_CHUNK):
        o_ref[:, c:c + MM_CHUNK] = _dot(xn, w_ref[:, c:c + MM_CHUNK]).astype(o_ref.dtype)


def _norm_mm(x, g, w):
    n, d = x.shape
    dout = w.shape[1]
    tm = min(512, n)
    assert dout % MM_CHUNK == 0
    return pl.pallas_call(
        _norm_mm_kernel,
        grid=(n // tm,),
        in_specs=[
            pl.BlockSpec((tm, d), lambda i: (i, 0)),
            _resident((1, d)),
            _resident((d, dout)),
        ],
        out_specs=pl.BlockSpec((tm, dout), lambda i: (i, 0)),
        out_shape=jax.ShapeDtypeStruct((n, dout), BF16),
        compiler_params=_params("parallel"),
        name="norm_mm",
    )(x, g.reshape(1, d), w)


def _mm_res_kernel(h_ref, a_ref, w_ref, o_ref):
    a = a_ref[...]
    for c in range(0, w_ref.shape[1], MM_CHUNK):
        o_ref[:, c:c + MM_CHUNK] = h_ref[:, c:c + MM_CHUNK] + _dot(a, w_ref[:, c:c + MM_CHUNK])


def _mm_res(h, a, w):
    n, d = h.shape
    tm = min(512, n)
    return pl.pallas_call(
        _mm_res_kernel,
        grid=(n // tm,),
        in_specs=[
            pl.BlockSpec((tm, d), lambda i: (i, 0)),
            pl.BlockSpec((tm, a.shape[1]), lambda i: (i, 0)),
            _resident(w.shape),
        ],
        out_specs=pl.BlockSpec((tm, d), lambda i: (i, 0)),
        out_shape=jax.ShapeDtypeStruct((n, d), F32),
        compiler_params=_params("parallel"),
        name="mm_res",
    )(h, a, w)


def _glu_kernel(h_ref, z_ref, w_ref, o_ref):
    d = h_ref.shape[1]
    z = z_ref[...]
    for c in range(0, d, MM_CHUNK):
        out = _dot(z, w_ref[:, c:c + MM_CHUNK])
        gate = _dot(z, w_ref[:, d + c:d + c + MM_CHUNK])
        o_ref[:, c:c + MM_CHUNK] = h_ref[:, c:c + MM_CHUNK] + out * jax.nn.sigmoid(gate)


def _glu(h, z, w):
    n, d = h.shape
    tm = min(512, n)
    return pl.pallas_call(
        _glu_kernel,
        grid=(n // tm,),
        in_specs=[
            pl.BlockSpec((tm, d), lambda i: (i, 0)),
            pl.BlockSpec((tm, d), lambda i: (i, 0)),
            _resident(w.shape),
        ],
        out_specs=pl.BlockSpec((tm, d), lambda i: (i, 0)),
        out_shape=jax.ShapeDtypeStruct((n, d), F32),
        compiler_params=_params("parallel"),
        name="glu",
    )(h, z, w)


def _ple_kernel(h_ref, p_ref, g_ref, wg_ref, wp_ref, fg_ref, o_ref, *, final):
    hn = _rms(h_ref[...], g_ref[...]).astype(BF16)
    pb = p_ref[0].astype(BF16)
    for c in range(0, wg_ref.shape[1], MM_CHUNK):
        gate = jax.nn.sigmoid(_dot(hn, wg_ref[:, c:c + MM_CHUNK]))
        o_ref[:, c:c + MM_CHUNK] = h_ref[:, c:c + MM_CHUNK] + _dot(pb, wp_ref[:, c:c + MM_CHUNK]) * gate
    if final:
        o_ref[...] = _rms(o_ref[...], fg_ref[...])


def _ple(h, p, layer, g, w_gate, w_proj, final_g, final):
    n, d = h.shape
    tm = min(512, n)
    return pl.pallas_call(
        functools.partial(_ple_kernel, final=final),
        grid=(n // tm,),
        in_specs=[
            pl.BlockSpec((tm, d), lambda i: (i, 0)),
            pl.BlockSpec((1, tm, p.shape[2]), lambda i: (layer, i, 0)),
            _resident((1, d)),
            _resident(w_gate.shape),
            _resident(w_proj.shape),
            _resident((1, d)),
        ],
        out_specs=pl.BlockSpec((tm, d), lambda i: (i, 0)),
        out_shape=jax.ShapeDtypeStruct((n, d), F32),
        compiler_params=_params("parallel"),
        name="ple",
    )(h, p, g.reshape(1, d), w_gate, w_proj, final_g.reshape(1, d))


def _s5_kernel(x_ref, k_ref, b_ref, c_ref, dec_ref, d_ref, z_ref, m_ref, sp_ref, *, nb, nc):
    hh = k_ref.shape[1]
    t = m_ref.shape[0] // hh
    ns = sp_ref.shape[1] // 2

    taps = k_ref[0]
    lane = lax.broadcasted_iota(jnp.int32, taps.shape, 1)
    m_ref[0:hh, :] = taps.astype(BF16)
    for s in range(1, t):
        shifted = jnp.where(lane >= s * hh, pltpu.roll(taps, s * hh, axis=1), 0.0)
        m_ref[s * hh:(s + 1) * hh, :] = shifted.astype(BF16)

    x = x_ref[0]
    s_in = _dot(x, b_ref[0])
    s_in_sw = pltpu.roll(s_in, ns, axis=1)
    dec = dec_ref[0]
    a_rr, a_s, a_sw = dec[0:1], dec[1:2], dec[2:3]
    st = jnp.zeros((nb, 2 * ns), F32)
    sw = jnp.zeros((nb, 2 * ns), F32)
    for c in range(nc):
        sp_ref[c * nb:(c + 1) * nb, :] = st
        rows = slice(c * nb, (c + 1) * nb)
        st, sw = (st * a_rr + sw * a_s + s_in[rows],
                  sw * a_rr + st * a_sw + s_in_sw[rows])

    y = _dot(x, m_ref[...]) + lax.dot_general(
        sp_ref[...].astype(BF16), c_ref[0], (((1,), (1,)), ((), ())), preferred_element_type=F32)
    y = y + d_ref[0] * x.astype(F32)
    z_ref[0] = jax.nn.gelu(y).astype(BF16)


def _s5_weights(a_re, a_im, log_dt, b_re, b_im, c_re, c_im, d_skip):
    t = S5_CHUNK
    g, ns, hh = b_re.shape
    hp = lax.Precision.HIGHEST
    lam_re = jnp.minimum(a_re.astype(F32), -1e-4)
    lam_im = a_im.astype(F32)
    dt = jnp.exp(log_dt.astype(F32))[:, None]
    mag = jnp.exp(lam_re * dt)
    abar_re = mag * jnp.cos(lam_im * dt)
    abar_im = mag * jnp.sin(lam_im * dt)
    den = lam_re * lam_re + lam_im * lam_im
    nr = abar_re - 1.0
    ni = abar_im
    fr = (nr * lam_re + ni * lam_im) / den
    fi = (ni * lam_re - nr * lam_im) / den
    bre = b_re.astype(F32)
    bim = b_im.astype(F32)
    bbar_re = fr[..., None] * bre - fi[..., None] * bim
    bbar_im = fr[..., None] * bim + fi[..., None] * bre
    cre = c_re.astype(F32)
    cim = c_im.astype(F32)

    k = jnp.arange(t + 1, dtype=F32)[None, :, None]
    pmag = jnp.exp(k * (lam_re * dt)[:, None, :])
    pw_re = pmag * jnp.cos(k * (lam_im * dt)[:, None, :])
    pw_im = pmag * jnp.sin(k * (lam_im * dt)[:, None, :])
    bt_re = bbar_re.transpose(0, 2, 1)
    bt_im = bbar_im.transpose(0, 2, 1)

    def c_times_power(w_re, w_im):
        return (cre[:, None] * w_re[:, :, None, :] - cim[:, None] * w_im[:, :, None, :],
                cre[:, None] * w_im[:, :, None, :] + cim[:, None] * w_re[:, :, None, :])

    cw_re, cw_im = c_times_power(pw_re[:, :t], pw_im[:, :t])
    cw_re = cw_re.reshape(g, t * hh, ns)
    cw_im = cw_im.reshape(g, t * hh, ns)
    taps = (jnp.einsum('gip,gnp->gin', bt_re, cw_re, precision=hp)
            - jnp.einsum('gip,gnp->gin', bt_im, cw_im, precision=hp))

    rev_re = pw_re[:, :t][:, ::-1]
    rev_im = pw_im[:, :t][:, ::-1]
    bm_re = rev_re[:, :, None, :] * bt_re[:, None] - rev_im[:, :, None, :] * bt_im[:, None]
    bm_im = rev_re[:, :, None, :] * bt_im[:, None] + rev_im[:, :, None, :] * bt_re[:, None]
    bmat = jnp.concatenate([bm_re, bm_im], axis=3).reshape(g, t * hh, 2 * ns)

    cm_re, cm_im = c_times_power(pw_re[:, 1:], pw_im[:, 1:])
    cmat = jnp.concatenate([cm_re, -cm_im], axis=3).reshape(g, t * hh, 2 * ns)

    at_re, at_im = pw_re[:, t], pw_im[:, t]
    dec = jnp.stack([
        jnp.concatenate([at_re, at_re], axis=1),
        jnp.concatenate([-at_im, at_im], axis=1),
        jnp.concatenate([at_im, -at_im], axis=1),
    ], axis=1)
    dec = jnp.pad(dec, ((0, 0), (0, 5), (0, 0)))

    dflat = jnp.tile(d_skip.astype(F32).reshape(g, 1, hh), (1, t, 1)).reshape(g, 1, t * hh)
    return taps, bmat.astype(BF16), cmat.astype(BF16), dec, dflat


def _s5_core(u, weights, b, l):
    taps, bmat, cmat, dec, dflat = weights
    t = S5_CHUNK
    g, hh = taps.shape[0], taps.shape[1]
    ns2 = bmat.shape[2]
    nc = l // t
    rows = nc * b
    th = t * hh
    x = u.reshape(b, nc, t, g, hh).transpose(3, 1, 0, 2, 4).reshape(g, rows, th)
    z = pl.pallas_call(
        functools.partial(_s5_kernel, nb=b, nc=nc),
        grid=(g,),
        in_specs=[
            pl.BlockSpec((1, rows, th), lambda i: (i, 0, 0)),
            pl.BlockSpec((1, hh, th), lambda i: (i, 0, 0)),
            pl.BlockSpec((1, th, ns2), lambda i: (i, 0, 0)),
            pl.BlockSpec((1, th, ns2), lambda i: (i, 0, 0)),
            pl.BlockSpec((1, 8, ns2), lambda i: (i, 0, 0)),
            pl.BlockSpec((1, 1, th), lambda i: (i, 0, 0)),
        ],
        out_specs=pl.BlockSpec((1, rows, th), lambda i: (i, 0, 0)),
        out_shape=jax.ShapeDtypeStruct((g, rows, th), BF16),
        scratch_shapes=[pltpu.VMEM((th, th), BF16), pltpu.VMEM((rows, ns2), F32)],
        compiler_params=_params("parallel"),
        name="s5_core",
    )(x, taps, bmat, cmat, dec, dflat)
    return z.reshape(g, nc, b, t, hh).transpose(2, 1, 3, 0, 4).reshape(b * l, g * hh)


def _sb_kernel(q_ref, k_ref, v_ref, o_ref, carry_ref, acc_ref):
    tk = SB_TK
    tq = carry_ref.shape[1]
    nsub = tq // tk
    nq = q_ref.shape[1] // tq
    two = 2 * LANES

    lane = lax.broadcasted_iota(jnp.int32, (tk, two), 1)
    row = lax.broadcasted_iota(jnp.int32, (tk, two), 0)
    strict = (lane & (LANES - 1)) < row
    trow = lax.broadcasted_iota(jnp.int32, (two, two), 0)
    tcol = lax.broadcasted_iota(jnp.int32, (two, two), 1)
    tri = jnp.where((tcol >= LANES) | ((trow & (LANES - 1)) > tcol), -1.0, 0.0).astype(BF16)
    head0 = lax.broadcasted_iota(jnp.int32, (tk, LANES), 1) < SB_HEAD_DIM

    def split_heads(t):
        zero = jnp.zeros_like(t)
        return jnp.concatenate([jnp.where(head0, t, zero), jnp.where(head0, zero, t)], axis=0)

    def mask_diag(a):
        top = jnp.where(strict, a[:tk], 0.0)
        return top if a.shape[0] == tk else jnp.concatenate([top, a[tk:]], axis=0)

    def scores(q_rows, kb, diag):
        z = lax.dot_general(q_rows, split_heads(kb), (((1,), (1,)), ((), ())),
                            preferred_element_type=F32)
        nlk = jnp.maximum(z, 0.0) + jnp.log(1.0 + jnp.exp(-jnp.abs(z)))
        ls = z - nlk
        if diag:
            nlk = mask_diag(nlk)
        hi = nlk.astype(BF16)
        lo = (nlk - hi.astype(F32)).astype(BF16)
        r = [_dot(jnp.concatenate([hi[:, h * LANES:(h + 1) * LANES],
                                   lo[:, h * LANES:(h + 1) * LANES]], axis=1), tri)
             for h in range(2)]
        return ls, r

    def attend(ls, r, carry, diag):
        arg = jnp.concatenate([r[h][:, :LANES] + carry[h] for h in range(2)], axis=1) + ls
        att = jnp.exp(arg)
        if diag:
            att = mask_diag(att)
        return att.astype(BF16), [carry[h] + r[h][:, LANES:] for h in range(2)]

    def q_tile(i, _):
        q0 = pl.multiple_of(i * tq, tq)
        q = q_ref[0, pl.ds(q0, tq), :]
        carry_ref[...] = jnp.zeros_like(carry_ref)
        acc_ref[...] = jnp.zeros_like(acc_ref)

        for kk in reversed(range(nsub)):
            r0 = kk * tk
            k0 = pl.multiple_of(q0 + r0, tk)
            ls, r = scores(q[r0:], k_ref[0, pl.ds(k0, tk), :], True)
            att, carry = attend(ls, r, [carry_ref[h, r0:, :] for h in range(2)], True)
            for h in range(2):
                carry_ref[h, r0:, :] = carry[h]
            acc_ref[r0:, :] += _dot(att, split_heads(v_ref[0, pl.ds(k0, tk), :]))

        def more_keys(state):
            jj, carry_max = state
            return (jj < i * (nsub // 2)) & (carry_max > SB_LOG_ZERO)

        def k_pair(state):
            jj, _ = state
            ka = pl.multiple_of(q0 - (2 * jj + 1) * tk, tk)
            kb = pl.multiple_of(q0 - (2 * jj + 2) * tk, tk)
            ls_a, r_a = scores(q, k_ref[0, pl.ds(ka, tk), :], False)
            ls_b, r_b = scores(q, k_ref[0, pl.ds(kb, tk), :], False)
            carry = [carry_ref[h] for h in range(2)]
            att_a, carry = attend(ls_a, r_a, carry, False)
            att_b, carry = attend(ls_b, r_b, carry, False)
            for h in range(2):
                carry_ref[h] = carry[h]
            vv = jnp.concatenate([split_heads(v_ref[0, pl.ds(ka, tk), :]),
                                  split_heads(v_ref[0, pl.ds(kb, tk), :])], axis=0)
            acc_ref[...] += _dot(jnp.concatenate([att_a, att_b], axis=1), vv)
            return jj + 1, jnp.max(jnp.maximum(carry[0], carry[1]))

        lax.while_loop(more_keys, k_pair, (jnp.int32(0), jnp.float32(0.0)))
        o_ref[0, pl.ds(q0, tq), :] = acc_ref[...].astype(o_ref.dtype)
        return 0

    lax.fori_loop(0, nq, q_tile, 0)


def _sb_attention(qkv, d):
    b, l, _ = qkv.shape
    nhp = d // LANES
    tq = min(SB_TQ, l)
    assert tq % (2 * SB_TK) == 0 and l % tq == 0
    return pl.pallas_call(
        _sb_kernel,
        grid=(b, nhp),
        in_specs=[
            pl.BlockSpec((1, l, LANES), lambda i, j: (i, 0, j)),
            pl.BlockSpec((1, l, LANES), lambda i, j: (i, 0, nhp + j)),
            pl.BlockSpec((1, l, LANES), lambda i, j: (i, 0, 2 * nhp + j)),
        ],
        out_specs=pl.BlockSpec((1, l, LANES), lambda i, j: (i, 0, j)),
        out_shape=jax.ShapeDtypeStruct((b, l, d), BF16),
        scratch_shapes=[pltpu.VMEM((2, tq, LANES), F32), pltpu.VMEM((tq, LANES), F32)],
        compiler_params=_params("parallel", "parallel"),
        name="sb_attention",
    )(qkv, qkv, qkv)


def kernel(x, p, ffn1_norm, ffn1_w1, ffn1_w3, ffn1_w2, mix_norm, ffn2_norm, ffn2_w1, ffn2_w3, ffn2_w2, ple_norm, ple_proj, ple_gate, s5_w_in, s5_a_re, s5_a_im, s5_log_dt, s5_b_re, s5_b_im, s5_c_re, s5_c_im, s5_d, s5_w_glu, sb_w_qkv, sb_w_o, final_norm):
    b, l, d = x.shape
    depth = p.shape[0]
    n = b * l
    bf = lambda w: w.astype(BF16)
    f1w1, f1w3, f1w2 = bf(ffn1_w1), bf(ffn1_w3), bf(ffn1_w2)
    f2w1, f2w3, f2w2 = bf(ffn2_w1), bf(ffn2_w3), bf(ffn2_w2)
    w_gate, w_proj = bf(ple_gate), bf(ple_proj)
    p = p.reshape(depth, n, p.shape[-1])
    h = x.reshape(n, d)
    for i in range(depth):
        h = _ffn(h, ffn1_norm[i], f1w1[i], f1w3[i], f1w2[i])
        j = i // 2
        if i % 2 == 0:
            u = _norm_mm(h, mix_norm[i], bf(s5_w_in[j]))
            weights = _s5_weights(s5_a_re[j], s5_a_im[j], s5_log_dt[j], s5_b_re[j], s5_b_im[j],
                                  s5_c_re[j], s5_c_im[j], s5_d[j])
            h = _glu(h, _s5_core(u, weights, b, l), bf(s5_w_glu[j]))
        else:
            col_scale = jnp.where(jnp.arange(3 * d) < d, SB_HEAD_DIM ** -0.5, 1.0).astype(F32)
            qkv = _norm_mm(h, mix_norm[i], bf(sb_w_qkv[j] * col_scale))
            o = _sb_attention(qkv.reshape(b, l, 3 * d), d)
            h = _mm_res(h, o.reshape(n, d), bf(sb_w_o[j]))
        h = _ffn(h, ffn2_norm[i], f2w1[i], f2w3[i], f2w2[i])
        h = _ple(h, p, i, ple_norm[i], w_gate[i], w_proj[i], final_norm, final=(i == depth - 1))
    return h.reshape(b, l, d)
```

```python
import functools

import jax
import jax.numpy as jnp
from jax import lax
from jax.experimental import pallas as pl
from jax.experimental.pallas import tpu as pltpu

F32 = jnp.float32
BF16 = jnp.bfloat16

EPS = 1e-6
S5_GROUP = 16
S5_STATE = 64
S5_CHUNK = 64
SB_HEAD_DIM = 64
LANES = 128
S5_T_LO = LANES // S5_GROUP
S5_T_HI = S5_CHUNK // S5_T_LO
S5_ROWS = 16
SB_TQ = 512
SB_TK = 128
SB_LOG2_ZERO = -150.0
LOG2_E = 1.4426950408889634
FFN_CHUNK = 256
FFN_ROWS = 512
MM_CHUNK = 512
VMEM_LIMIT_BYTES = 56 * 1024 * 1024


def _params(*semantics):
    return pltpu.CompilerParams(dimension_semantics=semantics,
                                vmem_limit_bytes=VMEM_LIMIT_BYTES)


def _rms(x, g):
    return x * lax.rsqrt(jnp.mean(x * x, axis=-1, keepdims=True) + EPS) * g


def _dot(a, b):
    return jnp.dot(a, b, preferred_element_type=F32)


def _swiglu_half_step(x, g_ref, w1_ref, w3_ref, w2_ref, o_ref):
    xn = _rms(x, g_ref[...]).astype(BF16)
    o_ref[...] = x
    dff = w1_ref.shape[1]
    for c in range(0, dff, FFN_CHUNK):
        a = jax.nn.silu(_dot(xn, w1_ref[:, c:c + FFN_CHUNK])) * _dot(xn, w3_ref[:, c:c + FFN_CHUNK])
        o_ref[...] += _dot((0.5 * a).astype(BF16), w2_ref[c:c + FFN_CHUNK, :])


def _ffn_kernel(x_ref, g_ref, w1_ref, w3_ref, w2_ref, o_ref):
    _swiglu_half_step(x_ref[...], g_ref, w1_ref, w3_ref, w2_ref, o_ref)


def _proj_ffn_kernel(h_ref, a_ref, wa_ref, g_ref, w1_ref, w3_ref, w2_ref, o_ref):
    x = h_ref[...] + _dot(a_ref[...], wa_ref[...])
    _swiglu_half_step(x, g_ref, w1_ref, w3_ref, w2_ref, o_ref)


def _ple_ffn_kernel(h_ref, p_ref, gp_ref, wg_ref, wp_ref, g_ref, w1_ref, w3_ref, w2_ref, o_ref):
    h = h_ref[...]
    gate = jax.nn.sigmoid(_dot(_rms(h, gp_ref[...]).astype(BF16), wg_ref[...]))
    x = h + _dot(p_ref[0].astype(BF16), wp_ref[...]) * gate
    _swiglu_half_step(x, g_ref, w1_ref, w3_ref, w2_ref, o_ref)


def _resident(shape):
    return pl.BlockSpec(shape, lambda *_: (0,) * len(shape), pipeline_mode=pl.Buffered(1))


def _ffn_call(body, lead, lead_specs, g, w1, w3, w2, name):
    n, d = lead[0].shape
    dff = w1.shape[1]
    assert dff % FFN_CHUNK == 0
    return pl.pallas_call(
        body,
        grid=(n // FFN_ROWS,),
        in_specs=lead_specs + [_resident((1, d)), _resident((d, dff)), _resident((d, dff)),
                               _resident((dff, d))],
        out_specs=pl.BlockSpec((FFN_ROWS, d), lambda i: (i, 0)),
        out_shape=jax.ShapeDtypeStruct((n, d), F32),
        compiler_params=_params("parallel"),
        name=name,
    )(*lead, g.reshape(1, d), w1, w3, w2)


def _rows(width):
    return pl.BlockSpec((FFN_ROWS, width), lambda i: (i, 0))


def _ffn(x, g, w1, w3, w2):
    return _ffn_call(_ffn_kernel, [x], [_rows(x.shape[1])], g, w1, w3, w2, "ffn")


def _proj_ffn(h, a, wa, g, w1, w3, w2):
    return _ffn_call(_proj_ffn_kernel, [h, a, wa],
                     [_rows(h.shape[1]), _rows(a.shape[1]), _resident(wa.shape)],
                     g, w1, w3, w2, "proj_ffn")


def _ple_ffn(h, p, layer, gp, w_gate, w_proj, g, w1, w3, w2):
    d = h.shape[1]
    p_spec = pl.BlockSpec((1, FFN_ROWS, p.shape[2]), lambda i: (layer, i, 0))
    return _ffn_call(_ple_ffn_kernel, [h, p, gp.reshape(1, d), w_gate, w_proj],
                     [_rows(d), p_spec, _resident((1, d)), _resident(w_gate.shape),
                      _resident(w_proj.shape)],
                     g, w1, w3, w2, "ple_ffn")


def _norm_mm_kernel(x_ref, g_ref, w_ref, o_ref):
    xn = _rms(x_ref[...], g_ref[...]).astype(BF16)
    for c in range(0, w_ref.shape[1], MM_CHUNK):
        o_ref[:, c:c + MM_CHUNK] = _dot(xn, w_ref[:, c:c + MM_CHUNK]).astype(o_ref.dtype)


def _norm_mm(x, g, w):
    n, d = x.shape
    dout = w.shape[1]
    tm = min(512, n)
    assert dout % MM_CHUNK == 0
    return pl.pallas_call(
        _norm_mm_kernel,
        grid=(n // tm,),
        in_specs=[
            pl.BlockSpec((tm, d), lambda i: (i, 0)),
            _resident((1, d)),
            _resident((d, dout)),
        ],
        out_specs=pl.BlockSpec((tm, dout), lambda i: (i, 0)),
        out_shape=jax.ShapeDtypeStruct((n, dout), BF16),
        compiler_params=_params("parallel"),
        name="norm_mm",
    )(x, g.reshape(1, d), w)


def _lane_permutation():
    a = jnp.arange(S5_T_LO * LANES)
    t_lo, g8, h = a // LANES, (a // S5_GROUP) % S5_T_LO, a % S5_GROUP
    b = g8 * LANES + t_lo * S5_GROUP + h
    return (b[:, None] == a[None, :]).astype(BF16)


def _frame_rows(j, t_hi, t_lo, rb):
    return pl.ds(j * rb * S5_CHUNK + t_hi * S5_T_LO + t_lo, rb, stride=S5_CHUNK)


def _s5_in_kernel(x_ref, g_ref, w_ref, perm_ref, z_ref, xs_ref):
    rb = z_ref.shape[2]
    tm = x_ref.shape[0]
    nj = x_ref.shape[1] // LANES
    blk = S5_T_HI * rb
    for j in range(nj):
        xs_ref[j * tm:(j + 1) * tm, :] = x_ref[:, j * LANES:(j + 1) * LANES]
    xperm = jnp.concatenate(
        [jnp.concatenate([xs_ref[_frame_rows(j, th, tl, rb), :] for j in range(nj)], axis=1)
         for tl in range(S5_T_LO) for th in range(S5_T_HI)], axis=0)
    u = _dot(_rms(xperm, g_ref[...]).astype(BF16), w_ref[...]).astype(BF16)
    lhs = jnp.concatenate(
        [jnp.concatenate([u[tl * blk:(tl + 1) * blk, j * LANES:(j + 1) * LANES]
                          for tl in range(S5_T_LO)], axis=1) for j in range(nj)], axis=0)
    zall = _dot(lhs, perm_ref[...]).astype(BF16)
    for j in range(nj):
        for th in range(S5_T_HI):
            r0 = (j * S5_T_HI + th) * rb
            for g8 in range(S5_T_LO):
                z_ref[j * S5_T_LO + g8, th, :, :] = zall[r0:r0 + rb, g8 * LANES:(g8 + 1) * LANES]


def _s5_in(x, g, w, perm, b, l):
    n, d = x.shape
    rb = S5_ROWS
    tm = rb * S5_CHUNK
    groups = d // S5_GROUP
    chunks = n // S5_CHUNK
    assert n % tm == 0 and l % S5_CHUNK == 0 and d % LANES == 0
    return pl.pallas_call(
        _s5_in_kernel,
        grid=(n // tm,),
        in_specs=[
            pl.BlockSpec((tm, d), lambda i: (i, 0)),
            _resident((1, d)),
            _resident(w.shape),
            _resident(perm.shape),
        ],
        out_specs=pl.BlockSpec((groups, S5_T_HI, rb, LANES), lambda i: (0, 0, i, 0)),
        out_shape=jax.ShapeDtypeStruct((groups, S5_T_HI, chunks, LANES), BF16),
        scratch_shapes=[pltpu.VMEM((d // LANES * tm, LANES), F32)],
        compiler_params=_params("parallel"),
        name="s5_in",
    )(x, g.reshape(1, d), w, perm)


def _s5_out_kernel(h_ref, z_ref, w_ref, permt_ref, o_ref, us_ref):
    rb = z_ref.shape[2]
    tm, d = h_ref.shape
    nj = d // LANES
    blk = S5_T_HI * rb
    lhs = jnp.concatenate(
        [jnp.concatenate([z_ref[j * S5_T_LO + g8, th] for g8 in range(S5_T_LO)], axis=1)
         for j in range(nj) for th in range(S5_T_HI)], axis=0)
    zp = _dot(lhs, permt_ref[...]).astype(BF16)
    z = jnp.concatenate(
        [jnp.concatenate([zp[j * blk:(j + 1) * blk, tl * LANES:(tl + 1) * LANES]
                          for j in range(nj)], axis=1) for tl in range(S5_T_LO)], axis=0)
    for c in range(0, d, MM_CHUNK):
        out = _dot(z, w_ref[:, c:c + MM_CHUNK])
        gate = _dot(z, w_ref[:, d + c:d + c + MM_CHUNK])
        upd = out * jax.nn.sigmoid(gate)
        for tl in range(S5_T_LO):
            for th in range(S5_T_HI):
                r0 = (tl * S5_T_HI + th) * rb
                for jj in range(MM_CHUNK // LANES):
                    us_ref[_frame_rows(c // LANES + jj, th, tl, rb), :] = (
                        upd[r0:r0 + rb, jj * LANES:(jj + 1) * LANES])
    for j in range(nj):
        o_ref[:, j * LANES:(j + 1) * LANES] = (
            h_ref[:, j * LANES:(j + 1) * LANES] + us_ref[j * tm:(j + 1) * tm, :])


def _s5_out(h, zg, w, permt):
    n, d = h.shape
    rb = S5_ROWS
    tm = rb * S5_CHUNK
    return pl.pallas_call(
        _s5_out_kernel,
        grid=(n // tm,),
        in_specs=[
            pl.BlockSpec((tm, d), lambda i: (i, 0)),
            pl.BlockSpec((zg.shape[0], S5_T_HI, rb, LANES), lambda i: (0, 0, i, 0)),
            _resident(w.shape),
            _resident(permt.shape),
        ],
        out_specs=pl.BlockSpec((tm, d), lambda i: (i, 0)),
        out_shape=jax.ShapeDtypeStruct((n, d), F32),
        scratch_shapes=[pltpu.VMEM((d // LANES * tm, LANES), F32)],
        compiler_params=_params("parallel"),
        name="s5_out",
    )(h, zg, w, permt)


def _ple_final_kernel(h_ref, p_ref, g_ref, wg_ref, wp_ref, fg_ref, o_ref):
    hn = _rms(h_ref[...], g_ref[...]).astype(BF16)
    pb = p_ref[0].astype(BF16)
    for c in range(0, wg_ref.shape[1], MM_CHUNK):
        gate = jax.nn.sigmoid(_dot(hn, wg_ref[:, c:c + MM_CHUNK]))
        o_ref[:, c:c + MM_CHUNK] = h_ref[:, c:c + MM_CHUNK] + _dot(pb, wp_ref[:, c:c + MM_CHUNK]) * gate
    o_ref[...] = _rms(o_ref[...], fg_ref[...])


def _ple_final(h, p, layer, g, w_gate, w_proj, final_g):
    n, d = h.shape
    tm = min(512, n)
    return pl.pallas_call(
        _ple_final_kernel,
        grid=(n // tm,),
        in_specs=[
            pl.BlockSpec((tm, d), lambda i: (i, 0)),
            pl.BlockSpec((1, tm, p.shape[2]), lambda i: (layer, i, 0)),
            _resident((1, d)),
            _resident(w_gate.shape),
            _resident(w_proj.shape),
            _resident((1, d)),
        ],
        out_specs=pl.BlockSpec((tm, d), lambda i: (i, 0)),
        out_shape=jax.ShapeDtypeStruct((n, d), F32),
        compiler_params=_params("parallel"),
        name="ple_final",
    )(h, p, g.reshape(1, d), w_gate, w_proj, final_g.reshape(1, d))


def _s5_kernel(x_ref, k_ref, b_ref, c_ref, dec_ref, d_ref, z_ref, m_ref, sin_ref, sw_ref, sp_ref,
               *, nb, nc):
    hh = k_ref.shape[1]
    t = m_ref.shape[0] // hh
    ns = sp_ref.shape[1] // 2

    taps = k_ref[0]
    lane = lax.broadcasted_iota(jnp.int32, taps.shape, 1)
    m_ref[0:hh, :] = taps.astype(BF16)
    for s in range(1, t):
        shifted = jnp.where(lane >= s * hh, pltpu.roll(taps, s * hh, axis=1), 0.0)
        m_ref[s * hh:(s + 1) * hh, :] = shifted.astype(BF16)

    x = jnp.concatenate([x_ref[0, th] for th in range(S5_T_HI)], axis=1)
    s_in = _dot(x, b_ref[0])
    sin_ref[...] = s_in
    sw_ref[...] = pltpu.roll(s_in, ns, axis=1)
    dec = dec_ref[0]
    a_rr, a_s, a_sw = dec[0:1], dec[1:2], dec[2:3]
    st = jnp.zeros((nb, 2 * ns), F32)
    sw = jnp.zeros((nb, 2 * ns), F32)
    for c in range(nc):
        rows = pl.ds(c, nb, stride=nc)
        sp_ref[rows, :] = st
        st, sw = (st * a_rr + sw * a_s + sin_ref[rows, :],
                  sw * a_rr + st * a_sw + sw_ref[rows, :])

    y = _dot(x, m_ref[...]) + lax.dot_general(
        sp_ref[...].astype(BF16), c_ref[0], (((1,), (1,)), ((), ())), preferred_element_type=F32)
    y = y + d_ref[0] * x.astype(F32)
    z = jax.nn.gelu(y).astype(BF16)
    for th in range(S5_T_HI):
        z_ref[0, th] = z[:, th * LANES:(th + 1) * LANES]


def _s5_weights(a_re, a_im, log_dt, b_re, b_im, c_re, c_im, d_skip):
    t = S5_CHUNK
    g, ns, hh = b_re.shape
    hp = lax.Precision.HIGHEST
    lam_re = jnp.minimum(a_re.astype(F32), -1e-4)
    lam_im = a_im.astype(F32)
    dt = jnp.exp(log_dt.astype(F32))[:, None]
    mag = jnp.exp(lam_re * dt)
    abar_re = mag * jnp.cos(lam_im * dt)
    abar_im = mag * jnp.sin(lam_im * dt)
    den = lam_re * lam_re + lam_im * lam_im
    nr = abar_re - 1.0
    ni = abar_im
    fr = (nr * lam_re + ni * lam_im) / den
    fi = (ni * lam_re - nr * lam_im) / den
    bre = b_re.astype(F32)
    bim = b_im.astype(F32)
    bbar_re = fr[..., None] * bre - fi[..., None] * bim
    bbar_im = fr[..., None] * bim + fi[..., None] * bre
    cre = c_re.astype(F32)
    cim = c_im.astype(F32)

    k = jnp.arange(t + 1, dtype=F32)[None, :, None]
    pmag = jnp.exp(k * (lam_re * dt)[:, None, :])
    pw_re = pmag * jnp.cos(k * (lam_im * dt)[:, None, :])
    pw_im = pmag * jnp.sin(k * (lam_im * dt)[:, None, :])
    bt_re = bbar_re.transpose(0, 2, 1)
    bt_im = bbar_im.transpose(0, 2, 1)

    def c_times_power(w_re, w_im):
        return (cre[:, None] * w_re[:, :, None, :] - cim[:, None] * w_im[:, :, None, :],
                cre[:, None] * w_im[:, :, None, :] + cim[:, None] * w_re[:, :, None, :])

    cw_re, cw_im = c_times_power(pw_re[:, :t], pw_im[:, :t])
    cw_re = cw_re.reshape(g, t * hh, ns)
    cw_im = cw_im.reshape(g, t * hh, ns)
    taps = (jnp.einsum('gip,gnp->gin', bt_re, cw_re, precision=hp)
            - jnp.einsum('gip,gnp->gin', bt_im, cw_im, precision=hp))

    rev_re = pw_re[:, :t][:, ::-1]
    rev_im = pw_im[:, :t][:, ::-1]
    bm_re = rev_re[:, :, None, :] * bt_re[:, None] - rev_im[:, :, None, :] * bt_im[:, None]
    bm_im = rev_re[:, :, None, :] * bt_im[:, None] + rev_im[:, :, None, :] * bt_re[:, None]
    bmat = jnp.concatenate([bm_re, bm_im], axis=3).reshape(g, t * hh, 2 * ns)

    cm_re, cm_im = c_times_power(pw_re[:, 1:], pw_im[:, 1:])
    cmat = jnp.concatenate([cm_re, -cm_im], axis=3).reshape(g, t * hh, 2 * ns)

    at_re, at_im = pw_re[:, t], pw_im[:, t]
    dec = jnp.stack([
        jnp.concatenate([at_re, at_re], axis=1),
        jnp.concatenate([-at_im, at_im], axis=1),
        jnp.concatenate([at_im, -at_im], axis=1),
    ], axis=1)
    dec = jnp.pad(dec, ((0, 0), (0, 5), (0, 0)))

    dflat = jnp.tile(d_skip.astype(F32).reshape(g, 1, hh), (1, t, 1)).reshape(g, 1, t * hh)
    return taps, bmat.astype(BF16), cmat.astype(BF16), dec, dflat


def _s5_core(zg, weights, b, l):
    taps, bmat, cmat, dec, dflat = weights
    g, hh = taps.shape[0], taps.shape[1]
    ns2 = bmat.shape[2]
    nc = l // S5_CHUNK
    rows = nc * b
    th = S5_CHUNK * hh
    blk = (1, S5_T_HI, rows, LANES)
    return pl.pallas_call(
        functools.partial(_s5_kernel, nb=b, nc=nc),
        grid=(g,),
        in_specs=[
            pl.BlockSpec(blk, lambda i: (i, 0, 0, 0)),
            pl.BlockSpec((1, hh, th), lambda i: (i, 0, 0)),
            pl.BlockSpec((1, th, ns2), lambda i: (i, 0, 0)),
            pl.BlockSpec((1, th, ns2), lambda i: (i, 0, 0)),
            pl.BlockSpec((1, 8, ns2), lambda i: (i, 0, 0)),
            pl.BlockSpec((1, 1, th), lambda i: (i, 0, 0)),
        ],
        out_specs=pl.BlockSpec(blk, lambda i: (i, 0, 0, 0)),
        out_shape=jax.ShapeDtypeStruct(zg.shape, BF16),
        scratch_shapes=[pltpu.VMEM((th, th), BF16), pltpu.VMEM((rows, ns2), F32),
                        pltpu.VMEM((rows, ns2), F32), pltpu.VMEM((rows, ns2), F32)],
        compiler_params=_params("parallel"),
        name="s5_core",
    )(zg, taps, bmat, cmat, dec, dflat)


def _sb_kernel(q_ref, k_ref, v_ref, o_ref, carry_ref, acc_ref):
    tk = SB_TK
    tq = carry_ref.shape[1]
    nsub = tq // tk
    nq = q_ref.shape[1] // tq
    two = 2 * LANES

    lane = lax.broadcasted_iota(jnp.int32, (tk, two), 1)
    row = lax.broadcasted_iota(jnp.int32, (tk, two), 0)
    strict = (lane & (LANES - 1)) < row
    trow = lax.broadcasted_iota(jnp.int32, (two, two), 0)
    tcol = lax.broadcasted_iota(jnp.int32, (two, two), 1)
    tri = jnp.where((tcol >= LANES) | ((trow & (LANES - 1)) > tcol), -1.0, 0.0).astype(BF16)
    head0 = lax.broadcasted_iota(jnp.int32, (tk, LANES), 1) < SB_HEAD_DIM

    def split_heads(t):
        zero = jnp.zeros_like(t)
        return jnp.concatenate([jnp.where(head0, t, zero), jnp.where(head0, zero, t)], axis=0)

    def mask_diag(a):
        top = jnp.where(strict, a[:tk], 0.0)
        return top if a.shape[0] == tk else jnp.concatenate([top, a[tk:]], axis=0)

    def qk(q_rows, kb):
        return lax.dot_general(q_rows, split_heads(kb), (((1,), (1,)), ((), ())),
                               preferred_element_type=F32)

    def logs(z, diag):
        nlk = jnp.maximum(z, 0.0) + jnp.log2(1.0 + jnp.exp2(-jnp.abs(z)))
        ls = z - nlk
        if diag:
            nlk = mask_diag(nlk)
        hi = nlk.astype(BF16)
        lo = (nlk - hi.astype(F32)).astype(BF16)
        r = [_dot(jnp.concatenate([hi[:, h * LANES:(h + 1) * LANES],
                                   lo[:, h * LANES:(h + 1) * LANES]], axis=1), tri)
             for h in range(2)]
        return ls, r

    def attend(ls, r, carry, diag):
        arg = jnp.concatenate([r[h][:, :LANES] + carry[h] for h in range(2)], axis=1) + ls
        att = jnp.exp2(arg)
        if diag:
            att = mask_diag(att)
        return att.astype(BF16), [carry[h] + r[h][:, LANES:] for h in range(2)]

    def q_tile(i, first):
        q0 = 0 if first else pl.multiple_of(i * tq, tq)
        q = q_ref[0, pl.ds(q0, tq), :]
        carry_ref[...] = jnp.zeros_like(carry_ref)
        acc_ref[...] = jnp.zeros_like(acc_ref)

        def sweep(tiles):
            zs = [qk(q[r0:], k_ref[0, pl.ds(k0, tk), :]) for r0, k0, _ in tiles]
            lrs = [logs(z, diag) for z, (_, _, diag) in zip(zs, tiles)]
            for (ls, r), (r0, k0, diag) in zip(lrs, tiles):
                carry = [carry_ref[h, r0:, :] for h in range(2)]
                att, carry = attend(ls, r, carry, diag)
                for h in range(2):
                    carry_ref[h, r0:, :] = carry[h]
                acc_ref[r0:, :] += _dot(att, split_heads(v_ref[0, pl.ds(k0, tk), :]))
            return jnp.max(jnp.maximum(carry[0], carry[1]))

        def pair(jj):
            return [(0, pl.multiple_of(q0 - (2 * jj + 1) * tk, tk), False),
                    (0, pl.multiple_of(q0 - (2 * jj + 2) * tk, tk), False)]

        diag_tiles = [(kk * tk, kk * tk if first else pl.multiple_of(q0 + kk * tk, tk), True)
                      for kk in reversed(range(nsub))]
        if first:
            sweep(diag_tiles)
        else:
            def more_keys(state):
                jj, carry_max = state
                return (jj < i * (nsub // 2)) & (carry_max > SB_LOG2_ZERO)

            lax.while_loop(more_keys, lambda state: (state[0] + 1, sweep(pair(state[0]))),
                           (jnp.int32(1), sweep(diag_tiles + pair(0))))
        o_ref[0, pl.ds(q0, tq), :] = acc_ref[...].astype(o_ref.dtype)

    q_tile(0, True)

    def later_q_tile(i, _):
        q_tile(i, False)
        return 0

    lax.fori_loop(1, nq, later_q_tile, 0)


def _sb_attention(qkv, d):
    b, l, _ = qkv.shape
    nhp = d // LANES
    tq = min(SB_TQ, l)
    assert tq % (2 * SB_TK) == 0 and l % tq == 0
    return pl.pallas_call(
        _sb_kernel,
        grid=(b, nhp),
        in_specs=[
            pl.BlockSpec((1, l, LANES), lambda i, j: (i, 0, j)),
            pl.BlockSpec((1, l, LANES), lambda i, j: (i, 0, nhp + j)),
            pl.BlockSpec((1, l, LANES), lambda i, j: (i, 0, 2 * nhp + j)),
        ],
        out_specs=pl.BlockSpec((1, l, LANES), lambda i, j: (i, 0, j)),
        out_shape=jax.ShapeDtypeStruct((b, l, d), BF16),
        scratch_shapes=[pltpu.VMEM((2, tq, LANES), F32), pltpu.VMEM((tq, LANES), F32)],
        compiler_params=_params("parallel", "parallel"),
        name="sb_attention",
    )(qkv, qkv, qkv)


def kernel(x, p, ffn1_norm, ffn1_w1, ffn1_w3, ffn1_w2, mix_norm, ffn2_norm, ffn2_w1, ffn2_w3, ffn2_w2, ple_norm, ple_proj, ple_gate, s5_w_in, s5_a_re, s5_a_im, s5_log_dt, s5_b_re, s5_b_im, s5_c_re, s5_c_im, s5_d, s5_w_glu, sb_w_qkv, sb_w_o, final_norm):
    b, l, d = x.shape
    depth = p.shape[0]
    n = b * l
    bf = lambda w: w.astype(BF16)
    f1w1, f1w3, f1w2 = bf(ffn1_w1), bf(ffn1_w3), bf(ffn1_w2)
    f2w1, f2w3, f2w2 = bf(ffn2_w1), bf(ffn2_w3), bf(ffn2_w2)
    w_gate, w_proj = bf(ple_gate), bf(ple_proj)
    p = p.reshape(depth, n, p.shape[-1])
    h = x.reshape(n, d)
    for i in range(depth):
        if i == 0:
            h = _ffn(h, ffn1_norm[i], f1w1[i], f1w3[i], f1w2[i])
        else:
            h = _ple_ffn(h, p, i - 1, ple_norm[i - 1], w_gate[i - 1], w_proj[i - 1],
                         ffn1_norm[i], f1w1[i], f1w3[i], f1w2[i])
        j = i // 2
        if i % 2 == 0:
            perm = _lane_permutation()
            zg = _s5_in(h, mix_norm[i], bf(s5_w_in[j]), perm, b, l)
            weights = _s5_weights(s5_a_re[j], s5_a_im[j], s5_log_dt[j], s5_b_re[j], s5_b_im[j],
                                  s5_c_re[j], s5_c_im[j], s5_d[j])
            h = _s5_out(h, _s5_core(zg, weights, b, l), bf(s5_w_glu[j]), perm.T)
            h = _ffn(h, ffn2_norm[i], f2w1[i], f2w3[i], f2w2[i])
        else:
            col_scale = jnp.where(jnp.arange(3 * d) < d, LOG2_E * SB_HEAD_DIM ** -0.5, 1.0).astype(F32)
            qkv = _norm_mm(h, mix_norm[i], bf(sb_w_qkv[j] * col_scale))
            o = _sb_attention(qkv.reshape(b, l, 3 * d), d)
            h = _proj_ffn(h, o.reshape(n, d), bf(sb_w_o[j]),
                          ffn2_norm[i], f2w1[i], f2w3[i], f2w2[i])
    last = depth - 1
    h = _ple_final(h, p, last, ple_norm[last], w_gate[last], w_proj[last], final_norm)
    return h.reshape(b, l, d)
```

```python
import functools

import jax
import jax.numpy as jnp
from jax import lax
from jax.experimental import pallas as pl
from jax.experimental.pallas import tpu as pltpu

F32 = jnp.float32
BF16 = jnp.bfloat16

EPS = 1e-6
S5_GROUP = 16
S5_STATE = 64
S5_CHUNK = 64
SB_HEAD_DIM = 64
LANES = 128
S5_T_LO = LANES // S5_GROUP
S5_T_HI = S5_CHUNK // S5_T_LO
S5_ROWS = 16
S5_PITCH = S5_CHUNK + 8
SB_TQ = 512
SB_TK = 128
SB_LOG2_ZERO = -150.0
LOG2_E = 1.4426950408889634
FFN_CHUNK = 256
FFN_ROWS = 512
MM_CHUNK = 512
VMEM_LIMIT_BYTES = 56 * 1024 * 1024


def _params(*semantics):
    return pltpu.CompilerParams(dimension_semantics=semantics,
                                vmem_limit_bytes=VMEM_LIMIT_BYTES)


def _rms(x, g):
    return x * lax.rsqrt(jnp.mean(x * x, axis=-1, keepdims=True) + EPS) * g


def _dot(a, b):
    return jnp.dot(a, b, preferred_element_type=F32)


def _swiglu_half_step(x, g_ref, w1_ref, w3_ref, w2_ref, o_ref):
    xn = _rms(x, g_ref[...]).astype(BF16)
    o_ref[...] = x
    dff = w1_ref.shape[1]
    for c in range(0, dff, FFN_CHUNK):
        a = jax.nn.silu(_dot(xn, w1_ref[:, c:c + FFN_CHUNK])) * _dot(xn, w3_ref[:, c:c + FFN_CHUNK])
        o_ref[...] += _dot((0.5 * a).astype(BF16), w2_ref[c:c + FFN_CHUNK, :])


def _ffn_kernel(x_ref, g_ref, w1_ref, w3_ref, w2_ref, o_ref):
    _swiglu_half_step(x_ref[...], g_ref, w1_ref, w3_ref, w2_ref, o_ref)


def _proj_ffn_kernel(h_ref, a_ref, wa_ref, g_ref, w1_ref, w3_ref, w2_ref, o_ref):
    x = h_ref[...] + _dot(a_ref[...], wa_ref[...])
    _swiglu_half_step(x, g_ref, w1_ref, w3_ref, w2_ref, o_ref)


def _ple_ffn_kernel(h_ref, p_ref, gp_ref, wg_ref, wp_ref, g_ref, w1_ref, w3_ref, w2_ref, o_ref):
    h = h_ref[...]
    gate = jax.nn.sigmoid(_dot(_rms(h, gp_ref[...]).astype(BF16), wg_ref[...]))
    x = h + _dot(p_ref[0].astype(BF16), wp_ref[...]) * gate
    _swiglu_half_step(x, g_ref, w1_ref, w3_ref, w2_ref, o_ref)


def _resident(shape):
    return pl.BlockSpec(shape, lambda *_: (0,) * len(shape), pipeline_mode=pl.Buffered(1))


def _ffn_call(body, lead, lead_specs, g, w1, w3, w2, name):
    n, d = lead[0].shape
    dff = w1.shape[1]
    assert dff % FFN_CHUNK == 0
    return pl.pallas_call(
        body,
        grid=(n // FFN_ROWS,),
        in_specs=lead_specs + [_resident((1, d)), _resident((d, dff)), _resident((d, dff)),
                               _resident((dff, d))],
        out_specs=pl.BlockSpec((FFN_ROWS, d), lambda i: (i, 0)),
        out_shape=jax.ShapeDtypeStruct((n, d), F32),
        compiler_params=_params("parallel"),
        name=name,
    )(*lead, g.reshape(1, d), w1, w3, w2)


def _rows(width):
    return pl.BlockSpec((FFN_ROWS, width), lambda i: (i, 0))


def _ffn(x, g, w1, w3, w2):
    return _ffn_call(_ffn_kernel, [x], [_rows(x.shape[1])], g, w1, w3, w2, "ffn")


def _proj_ffn(h, a, wa, g, w1, w3, w2):
    return _ffn_call(_proj_ffn_kernel, [h, a, wa],
                     [_rows(h.shape[1]), _rows(a.shape[1]), _resident(wa.shape)],
                     g, w1, w3, w2, "proj_ffn")


def _ple_ffn(h, p, layer, gp, w_gate, w_proj, g, w1, w3, w2):
    d = h.shape[1]
    p_spec = pl.BlockSpec((1, FFN_ROWS, p.shape[2]), lambda i: (layer, i, 0))
    return _ffn_call(_ple_ffn_kernel, [h, p, gp.reshape(1, d), w_gate, w_proj],
                     [_rows(d), p_spec, _resident((1, d)), _resident(w_gate.shape),
                      _resident(w_proj.shape)],
                     g, w1, w3, w2, "ple_ffn")


def _norm_mm_kernel(x_ref, g_ref, w_ref, o_ref):
    xn = _rms(x_ref[...], g_ref[...]).astype(BF16)
    for c in range(0, w_ref.shape[1], MM_CHUNK):
        o_ref[:, c:c + MM_CHUNK] = _dot(xn, w_ref[:, c:c + MM_CHUNK]).astype(o_ref.dtype)


def _norm_mm(x, g, w):
    n, d = x.shape
    dout = w.shape[1]
    tm = min(512, n)
    assert dout % MM_CHUNK == 0
    return pl.pallas_call(
        _norm_mm_kernel,
        grid=(n // tm,),
        in_specs=[
            pl.BlockSpec((tm, d), lambda i: (i, 0)),
            _resident((1, d)),
            _resident((d, dout)),
        ],
        out_specs=pl.BlockSpec((tm, dout), lambda i: (i, 0)),
        out_shape=jax.ShapeDtypeStruct((n, dout), BF16),
        compiler_params=_params("parallel"),
        name="norm_mm",
    )(x, g.reshape(1, d), w)


def _lane_permutation():
    a = jnp.arange(S5_T_LO * LANES)
    t_lo, g8, h = a // LANES, (a // S5_GROUP) % S5_T_LO, a % S5_GROUP
    b = g8 * LANES + t_lo * S5_GROUP + h
    return (b[:, None] == a[None, :]).astype(BF16)


def _frame_rows(j, t_hi, t_lo, rb):
    return pl.ds(j * rb * S5_PITCH + t_hi * S5_T_LO + t_lo, rb, stride=S5_PITCH)


def _chunk_rows(j, r, rb):
    return pl.ds((j * rb + r) * S5_PITCH, S5_CHUNK)


def _s5_in_kernel(x_ref, g_ref, w_ref, perm_ref, z_ref, xs_ref):
    rb = z_ref.shape[2]
    nj = x_ref.shape[1] // LANES
    blk = S5_T_HI * rb
    for j in range(nj):
        for r in range(rb):
            xs_ref[_chunk_rows(j, r, rb), :] = x_ref[r * S5_CHUNK:(r + 1) * S5_CHUNK,
                                                     j * LANES:(j + 1) * LANES]
    xperm = jnp.concatenate(
        [jnp.concatenate([xs_ref[_frame_rows(j, th, tl, rb), :] for j in range(nj)], axis=1)
         for tl in range(S5_T_LO) for th in range(S5_T_HI)], axis=0)
    u = _dot(_rms(xperm, g_ref[...]).astype(BF16), w_ref[...]).astype(BF16)
    lhs = jnp.concatenate(
        [jnp.concatenate([u[tl * blk:(tl + 1) * blk, j * LANES:(j + 1) * LANES]
                          for tl in range(S5_T_LO)], axis=1) for j in range(nj)], axis=0)
    zall = _dot(lhs, perm_ref[...]).astype(BF16)
    for j in range(nj):
        for th in range(S5_T_HI):
            r0 = (j * S5_T_HI + th) * rb
            for g8 in range(S5_T_LO):
                z_ref[j * S5_T_LO + g8, th, :, :] = zall[r0:r0 + rb, g8 * LANES:(g8 + 1) * LANES]


def _s5_in(x, g, w, perm, b, l):
    n, d = x.shape
    rb = S5_ROWS
    tm = rb * S5_CHUNK
    groups = d // S5_GROUP
    chunks = n // S5_CHUNK
    assert n % tm == 0 and l % S5_CHUNK == 0 and d % LANES == 0
    return pl.pallas_call(
        _s5_in_kernel,
        grid=(n // tm,),
        in_specs=[
            pl.BlockSpec((tm, d), lambda i: (i, 0)),
            _resident((1, d)),
            _resident(w.shape),
            _resident(perm.shape),
        ],
        out_specs=pl.BlockSpec((groups, S5_T_HI, rb, LANES), lambda i: (0, 0, i, 0)),
        out_shape=jax.ShapeDtypeStruct((groups, S5_T_HI, chunks, LANES), BF16),
        scratch_shapes=[pltpu.VMEM((d // LANES * rb * S5_PITCH, LANES), F32)],
        compiler_params=_params("parallel"),
        name="s5_in",
    )(x, g.reshape(1, d), w, perm)


def _s5_out_kernel(h_ref, z_ref, w_ref, permt_ref, o_ref, us_ref):
    rb = z_ref.shape[2]
    tm, d = h_ref.shape
    nj = d // LANES
    blk = S5_T_HI * rb
    lhs = jnp.concatenate(
        [jnp.concatenate([z_ref[j * S5_T_LO + g8, th] for g8 in range(S5_T_LO)], axis=1)
         for j in range(nj) for th in range(S5_T_HI)], axis=0)
    zp = _dot(lhs, permt_ref[...]).astype(BF16)
    z = jnp.concatenate(
        [jnp.concatenate([zp[j * blk:(j + 1) * blk, tl * LANES:(tl + 1) * LANES]
                          for j in range(nj)], axis=1) for tl in range(S5_T_LO)], axis=0)
    for c in range(0, d, MM_CHUNK):
        out = _dot(z, w_ref[:, c:c + MM_CHUNK])
        gate = _dot(z, w_ref[:, d + c:d + c + MM_CHUNK])
        upd = out * jax.nn.sigmoid(gate)
        for tl in range(S5_T_LO):
            for th in range(S5_T_HI):
                r0 = (tl * S5_T_HI + th) * rb
                for jj in range(MM_CHUNK // LANES):
                    us_ref[_frame_rows(c // LANES + jj, th, tl, rb), :] = (
                        upd[r0:r0 + rb, jj * LANES:(jj + 1) * LANES])
    for j in range(nj):
        for r in range(rb):
            rows = slice(r * S5_CHUNK, (r + 1) * S5_CHUNK)
            lanes = slice(j * LANES, (j + 1) * LANES)
            o_ref[rows, lanes] = h_ref[rows, lanes] + us_ref[_chunk_rows(j, r, rb), :]


def _s5_out(h, zg, w, permt):
    n, d = h.shape
    rb = S5_ROWS
    tm = rb * S5_CHUNK
    return pl.pallas_call(
        _s5_out_kernel,
        grid=(n // tm,),
        in_specs=[
            pl.BlockSpec((tm, d), lambda i: (i, 0)),
            pl.BlockSpec((zg.shape[0], S5_T_HI, rb, LANES), lambda i: (0, 0, i, 0)),
            _resident(w.shape),
            _resident(permt.shape),
        ],
        out_specs=pl.BlockSpec((tm, d), lambda i: (i, 0)),
        out_shape=jax.ShapeDtypeStruct((n, d), F32),
        scratch_shapes=[pltpu.VMEM((d // LANES * rb * S5_PITCH, LANES), F32)],
        compiler_params=_params("parallel"),
        name="s5_out",
    )(h, zg, w, permt)


def _ple_final_kernel(h_ref, p_ref, g_ref, wg_ref, wp_ref, fg_ref, o_ref):
    hn = _rms(h_ref[...], g_ref[...]).astype(BF16)
    pb = p_ref[0].astype(BF16)
    for c in range(0, wg_ref.shape[1], MM_CHUNK):
        gate = jax.nn.sigmoid(_dot(hn, wg_ref[:, c:c + MM_CHUNK]))
        o_ref[:, c:c + MM_CHUNK] = h_ref[:, c:c + MM_CHUNK] + _dot(pb, wp_ref[:, c:c + MM_CHUNK]) * gate
    o_ref[...] = _rms(o_ref[...], fg_ref[...])


def _ple_final(h, p, layer, g, w_gate, w_proj, final_g):
    n, d = h.shape
    tm = min(512, n)
    return pl.pallas_call(
        _ple_final_kernel,
        grid=(n // tm,),
        in_specs=[
            pl.BlockSpec((tm, d), lambda i: (i, 0)),
            pl.BlockSpec((1, tm, p.shape[2]), lambda i: (layer, i, 0)),
            _resident((1, d)),
            _resident(w_gate.shape),
            _resident(w_proj.shape),
            _resident((1, d)),
        ],
        out_specs=pl.BlockSpec((tm, d), lambda i: (i, 0)),
        out_shape=jax.ShapeDtypeStruct((n, d), F32),
        compiler_params=_params("parallel"),
        name="ple_final",
    )(h, p, g.reshape(1, d), w_gate, w_proj, final_g.reshape(1, d))


def _s5_kernel(x_ref, k_ref, b_ref, c_ref, dec_ref, d_ref, z_ref, m_ref, sin_ref, sw_ref, sp_ref,
               *, nb, nc):
    hh = k_ref.shape[1]
    t = m_ref.shape[0] // hh
    ns = sp_ref.shape[1] // 2

    taps = k_ref[0]
    lane = lax.broadcasted_iota(jnp.int32, taps.shape, 1)
    m_ref[0:hh, :] = taps.astype(BF16)
    for s in range(1, t):
        shifted = jnp.where(lane >= s * hh, pltpu.roll(taps, s * hh, axis=1), 0.0)
        m_ref[s * hh:(s + 1) * hh, :] = shifted.astype(BF16)

    x = jnp.concatenate([x_ref[0, th] for th in range(S5_T_HI)], axis=1)
    s_in = _dot(x, b_ref[0])
    s_in_sw = pltpu.roll(s_in, ns, axis=1)
    pitch = sp_ref.shape[0] // nb
    for b in range(nb):
        sin_ref[b * pitch:b * pitch + nc, :] = s_in[b * nc:(b + 1) * nc]
        sw_ref[b * pitch:b * pitch + nc, :] = s_in_sw[b * nc:(b + 1) * nc]
    dec = dec_ref[0]
    a_rr, a_s, a_sw = dec[0:1], dec[1:2], dec[2:3]
    st = jnp.zeros((nb, 2 * ns), F32)
    sw = jnp.zeros((nb, 2 * ns), F32)
    for c in range(nc):
        rows = pl.ds(c, nb, stride=pitch)
        sp_ref[rows, :] = st
        st, sw = (st * a_rr + sw * a_s + sin_ref[rows, :],
                  sw * a_rr + st * a_sw + sw_ref[rows, :])
    s_prev = jnp.concatenate([sp_ref[b * pitch:b * pitch + nc, :] for b in range(nb)], axis=0)

    y = _dot(x, m_ref[...]) + lax.dot_general(
        s_prev.astype(BF16), c_ref[0], (((1,), (1,)), ((), ())), preferred_element_type=F32)
    y = y + d_ref[0] * x.astype(F32)
    z = jax.nn.gelu(y).astype(BF16)
    for th in range(S5_T_HI):
        z_ref[0, th] = z[:, th * LANES:(th + 1) * LANES]


def _s5_weights(a_re, a_im, log_dt, b_re, b_im, c_re, c_im, d_skip):
    t = S5_CHUNK
    g, ns, hh = b_re.shape
    hp = lax.Precision.HIGHEST
    lam_re = jnp.minimum(a_re.astype(F32), -1e-4)
    lam_im = a_im.astype(F32)
    dt = jnp.exp(log_dt.astype(F32))[:, None]
    mag = jnp.exp(lam_re * dt)
    abar_re = mag * jnp.cos(lam_im * dt)
    abar_im = mag * jnp.sin(lam_im * dt)
    den = lam_re * lam_re + lam_im * lam_im
    nr = abar_re - 1.0
    ni = abar_im
    fr = (nr * lam_re + ni * lam_im) / den
    fi = (ni * lam_re - nr * lam_im) / den
    bre = b_re.astype(F32)
    bim = b_im.astype(F32)
    bbar_re = fr[..., None] * bre - fi[..., None] * bim
    bbar_im = fr[..., None] * bim + fi[..., None] * bre
    cre = c_re.astype(F32)
    cim = c_im.astype(F32)

    k = jnp.arange(t + 1, dtype=F32)[None, :, None]
    pmag = jnp.exp(k * (lam_re * dt)[:, None, :])
    pw_re = pmag * jnp.cos(k * (lam_im * dt)[:, None, :])
    pw_im = pmag * jnp.sin(k * (lam_im * dt)[:, None, :])
    bt_re = bbar_re.transpose(0, 2, 1)
    bt_im = bbar_im.transpose(0, 2, 1)

    def c_times_power(w_re, w_im):
        return (cre[:, None] * w_re[:, :, None, :] - cim[:, None] * w_im[:, :, None, :],
                cre[:, None] * w_im[:, :, None, :] + cim[:, None] * w_re[:, :, None, :])

    cw_re, cw_im = c_times_power(pw_re[:, :t], pw_im[:, :t])
    cw_re = cw_re.reshape(g, t * hh, ns)
    cw_im = cw_im.reshape(g, t * hh, ns)
    taps = (jnp.einsum('gip,gnp->gin', bt_re, cw_re, precision=hp)
            - jnp.einsum('gip,gnp->gin', bt_im, cw_im, precision=hp))

    rev_re = pw_re[:, :t][:, ::-1]
    rev_im = pw_im[:, :t][:, ::-1]
    bm_re = rev_re[:, :, None, :] * bt_re[:, None] - rev_im[:, :, None, :] * bt_im[:, None]
    bm_im = rev_re[:, :, None, :] * bt_im[:, None] + rev_im[:, :, None, :] * bt_re[:, None]
    bmat = jnp.concatenate([bm_re, bm_im], axis=3).reshape(g, t * hh, 2 * ns)

    cm_re, cm_im = c_times_power(pw_re[:, 1:], pw_im[:, 1:])
    cmat = jnp.concatenate([cm_re, -cm_im], axis=3).reshape(g, t * hh, 2 * ns)

    at_re, at_im = pw_re[:, t], pw_im[:, t]
    dec = jnp.stack([
        jnp.concatenate([at_re, at_re], axis=1),
        jnp.concatenate([-at_im, at_im], axis=1),
        jnp.concatenate([at_im, -at_im], axis=1),
    ], axis=1)
    dec = jnp.pad(dec, ((0, 0), (0, 5), (0, 0)))

    dflat = jnp.tile(d_skip.astype(F32).reshape(g, 1, hh), (1, t, 1)).reshape(g, 1, t * hh)
    return taps, bmat.astype(BF16), cmat.astype(BF16), dec, dflat


def _s5_core(zg, weights, b, l):
    taps, bmat, cmat, dec, dflat = weights
    g, hh = taps.shape[0], taps.shape[1]
    ns2 = bmat.shape[2]
    nc = l // S5_CHUNK
    rows = nc * b
    th = S5_CHUNK * hh
    blk = (1, S5_T_HI, rows, LANES)
    return pl.pallas_call(
        functools.partial(_s5_kernel, nb=b, nc=nc),
        grid=(g,),
        in_specs=[
            pl.BlockSpec(blk, lambda i: (i, 0, 0, 0)),
            pl.BlockSpec((1, hh, th), lambda i: (i, 0, 0)),
            pl.BlockSpec((1, th, ns2), lambda i: (i, 0, 0)),
            pl.BlockSpec((1, th, ns2), lambda i: (i, 0, 0)),
            pl.BlockSpec((1, 8, ns2), lambda i: (i, 0, 0)),
            pl.BlockSpec((1, 1, th), lambda i: (i, 0, 0)),
        ],
        out_specs=pl.BlockSpec(blk, lambda i: (i, 0, 0, 0)),
        out_shape=jax.ShapeDtypeStruct(zg.shape, BF16),
        scratch_shapes=[pltpu.VMEM((th, th), BF16)] + [pltpu.VMEM((b * (nc + 8), ns2), F32)] * 3,
        compiler_params=_params("parallel"),
        name="s5_core",
    )(zg, taps, bmat, cmat, dec, dflat)


def _sb_kernel(q_ref, k_ref, v_ref, o_ref, carry_ref, acc_ref):
    tk = SB_TK
    tq = carry_ref.shape[1]
    nsub = tq // tk
    nq = q_ref.shape[1] // tq
    two = 2 * LANES

    lane = lax.broadcasted_iota(jnp.int32, (tk, two), 1)
    row = lax.broadcasted_iota(jnp.int32, (tk, two), 0)
    strict = (lane & (LANES - 1)) < row
    trow = lax.broadcasted_iota(jnp.int32, (two, two), 0)
    tcol = lax.broadcasted_iota(jnp.int32, (two, two), 1)
    tri = jnp.where((tcol >= LANES) | ((trow & (LANES - 1)) > tcol), -1.0, 0.0).astype(BF16)
    head0 = lax.broadcasted_iota(jnp.int32, (tk, LANES), 1) < SB_HEAD_DIM

    def split_heads(t):
        zero = jnp.zeros_like(t)
        return jnp.concatenate([jnp.where(head0, t, zero), jnp.where(head0, zero, t)], axis=0)

    def mask_diag(a):
        top = jnp.where(strict, a[:tk], 0.0)
        return top if a.shape[0] == tk else jnp.concatenate([top, a[tk:]], axis=0)

    def qk(q_rows, kb):
        return lax.dot_general(q_rows, split_heads(kb), (((1,), (1,)), ((), ())),
                               preferred_element_type=F32)

    def logs(z, diag):
        nlk = jnp.maximum(z, 0.0) + jnp.log2(1.0 + jnp.exp2(-jnp.abs(z)))
        ls = z - nlk
        if diag:
            nlk = mask_diag(nlk)
        hi = nlk.astype(BF16)
        lo = (nlk - hi.astype(F32)).astype(BF16)
        r = [_dot(jnp.concatenate([hi[:, h * LANES:(h + 1) * LANES],
                                   lo[:, h * LANES:(h + 1) * LANES]], axis=1), tri)
             for h in range(2)]
        return ls, r

    def attend(ls, r, carry, diag):
        arg = jnp.concatenate([r[h][:, :LANES] + carry[h] for h in range(2)], axis=1) + ls
        att = jnp.exp2(arg)
        if diag:
            att = mask_diag(att)
        return att.astype(BF16), [carry[h] + r[h][:, LANES:] for h in range(2)]

    def q_tile(i, first):
        q0 = 0 if first else pl.multiple_of(i * tq, tq)
        q = q_ref[0, pl.ds(q0, tq), :]
        carry_ref[...] = jnp.zeros_like(carry_ref)
        acc_ref[...] = jnp.zeros_like(acc_ref)

        def sweep(tiles):
            zs = [qk(q[r0:], k_ref[0, pl.ds(k0, tk), :]) for r0, k0, _ in tiles]
            lrs = [logs(z, diag) for z, (_, _, diag) in zip(zs, tiles)]
            for (ls, r), (r0, k0, diag) in zip(lrs, tiles):
                carry = [carry_ref[h, r0:, :] for h in range(2)]
                att, carry = attend(ls, r, carry, diag)
                for h in range(2):
                    carry_ref[h, r0:, :] = carry[h]
                acc_ref[r0:, :] += _dot(att, split_heads(v_ref[0, pl.ds(k0, tk), :]))
            return jnp.max(jnp.maximum(carry[0], carry[1]))

        def pair(jj):
            return [(0, pl.multiple_of(q0 - (2 * jj + 1) * tk, tk), False),
                    (0, pl.multiple_of(q0 - (2 * jj + 2) * tk, tk), False)]

        diag_tiles = [(kk * tk, kk * tk if first else pl.multiple_of(q0 + kk * tk, tk), True)
                      for kk in reversed(range(nsub))]
        if first:
            sweep(diag_tiles)
        else:
            def more_keys(state):
                jj, carry_max = state
                return (jj < i * (nsub // 2)) & (carry_max > SB_LOG2_ZERO)

            lax.while_loop(more_keys, lambda state: (state[0] + 1, sweep(pair(state[0]))),
                           (jnp.int32(1), sweep(diag_tiles + pair(0))))
        o_ref[0, pl.ds(q0, tq), :] = acc_ref[...].astype(o_ref.dtype)

    q_tile(0, True)

    def later_q_tile(i, _):
        q_tile(i, False)
        return 0

    lax.fori_loop(1, nq, later_q_tile, 0)


def _sb_attention(qkv, d):
    b, l, _ = qkv.shape
    nhp = d // LANES
    tq = min(SB_TQ, l)
    assert tq % (2 * SB_TK) == 0 and l % tq == 0
    return pl.pallas_call(
        _sb_kernel,
        grid=(b, nhp),
        in_specs=[
            pl.BlockSpec((1, l, LANES), lambda i, j: (i, 0, j)),
            pl.BlockSpec((1, l, LANES), lambda i, j: (i, 0, nhp + j)),
            pl.BlockSpec((1, l, LANES), lambda i, j: (i, 0, 2 * nhp + j)),
        ],
        out_specs=pl.BlockSpec((1, l, LANES), lambda i, j: (i, 0, j)),
        out_shape=jax.ShapeDtypeStruct((b, l, d), BF16),
        scratch_shapes=[pltpu.VMEM((2, tq, LANES), F32), pltpu.VMEM((tq, LANES), F32)],
        compiler_params=_params("parallel", "parallel"),
        name="sb_attention",
    )(qkv, qkv, qkv)


def kernel(x, p, ffn1_norm, ffn1_w1, ffn1_w3, ffn1_w2, mix_norm, ffn2_norm, ffn2_w1, ffn2_w3, ffn2_w2, ple_norm, ple_proj, ple_gate, s5_w_in, s5_a_re, s5_a_im, s5_log_dt, s5_b_re, s5_b_im, s5_c_re, s5_c_im, s5_d, s5_w_glu, sb_w_qkv, sb_w_o, final_norm):
    b, l, d = x.shape
    depth = p.shape[0]
    n = b * l
    bf = lambda w: w.astype(BF16)
    f1w1, f1w3, f1w2 = bf(ffn1_w1), bf(ffn1_w3), bf(ffn1_w2)
    f2w1, f2w3, f2w2 = bf(ffn2_w1), bf(ffn2_w3), bf(ffn2_w2)
    w_gate, w_proj = bf(ple_gate), bf(ple_proj)
    p = p.reshape(depth, n, p.shape[-1])
    h = x.reshape(n, d)
    for i in range(depth):
        if i == 0:
            h = _ffn(h, ffn1_norm[i], f1w1[i], f1w3[i], f1w2[i])
        else:
            h = _ple_ffn(h, p, i - 1, ple_norm[i - 1], w_gate[i - 1], w_proj[i - 1],
                         ffn1_norm[i], f1w1[i], f1w3[i], f1w2[i])
        j = i // 2
        if i % 2 == 0:
            perm = _lane_permutation()
            zg = _s5_in(h, mix_norm[i], bf(s5_w_in[j]), perm, b, l)
            weights = _s5_weights(s5_a_re[j], s5_a_im[j], s5_log_dt[j], s5_b_re[j], s5_b_im[j],
                                  s5_c_re[j], s5_c_im[j], s5_d[j])
            h = _s5_out(h, _s5_core(zg, weights, b, l), bf(s5_w_glu[j]), perm.T)
            h = _ffn(h, ffn2_norm[i], f2w1[i], f2w3[i], f2w2[i])
        else:
            col_scale = jnp.where(jnp.arange(3 * d) < d, LOG2_E * SB_HEAD_DIM ** -0.5, 1.0).astype(F32)
            qkv = _norm_mm(h, mix_norm[i], bf(sb_w_qkv[j] * col_scale))
            o = _sb_attention(qkv.reshape(b, l, 3 * d), d)
            h = _proj_ffn(h, o.reshape(n, d), bf(sb_w_o[j]),
                          ffn2_norm[i], f2w1[i], f2w3[i], f2w2[i])
    last = depth - 1
    h = _ple_final(h, p, last, ple_norm[last], w_gate[last], w_proj[last], final_norm)
    return h.reshape(b, l, d)
```

```python
import functools

import jax
import jax.numpy as jnp
from jax import lax
from jax.experimental import pallas as pl
from jax.experimental.pallas import tpu as pltpu

F32 = jnp.float32
BF16 = jnp.bfloat16

EPS = 1e-6
S5_GROUP = 16
S5_STATE = 64
S5_CHUNK = 64
SB_HEAD_DIM = 64
LANES = 128
S5_T_LO = LANES // S5_GROUP
S5_T_HI = S5_CHUNK // S5_T_LO
S5_ROWS = 16
S5_PITCH = S5_CHUNK + 8
SB_TQ = 512
SB_TK = 128
SB_LOG2_ZERO = -150.0
LOG2_E = 1.4426950408889634
FFN_CHUNK = 256
FFN_ROWS = 512
MM_CHUNK = 512
VMEM_LIMIT_BYTES = 56 * 1024 * 1024


def _params(*semantics):
    return pltpu.CompilerParams(dimension_semantics=semantics,
                                vmem_limit_bytes=VMEM_LIMIT_BYTES)


def _rms(x, g):
    return x * lax.rsqrt(jnp.mean(x * x, axis=-1, keepdims=True) + EPS) * g


def _dot(a, b):
    return jnp.dot(a, b, preferred_element_type=F32)


def _swiglu_half_step(x, g_ref, w1_ref, w3_ref, w2_ref, o_ref):
    xn = _rms(x, g_ref[...]).astype(BF16)
    o_ref[...] = x
    dff = w1_ref.shape[1]
    for c in range(0, dff, FFN_CHUNK):
        a = jax.nn.silu(_dot(xn, w1_ref[:, c:c + FFN_CHUNK])) * _dot(xn, w3_ref[:, c:c + FFN_CHUNK])
        o_ref[...] += _dot((0.5 * a).astype(BF16), w2_ref[c:c + FFN_CHUNK, :])


def _ffn_kernel(x_ref, g_ref, w1_ref, w3_ref, w2_ref, o_ref):
    _swiglu_half_step(x_ref[...], g_ref, w1_ref, w3_ref, w2_ref, o_ref)


def _proj_ffn_kernel(h_ref, a_ref, wa_ref, g_ref, w1_ref, w3_ref, w2_ref, o_ref):
    x = h_ref[...] + _dot(a_ref[...], wa_ref[...])
    _swiglu_half_step(x, g_ref, w1_ref, w3_ref, w2_ref, o_ref)


def _ple_ffn_kernel(h_ref, p_ref, gp_ref, wg_ref, wp_ref, g_ref, w1_ref, w3_ref, w2_ref, o_ref):
    h = h_ref[...]
    gate = jax.nn.sigmoid(_dot(_rms(h, gp_ref[...]).astype(BF16), wg_ref[...]))
    x = h + _dot(p_ref[0].astype(BF16), wp_ref[...]) * gate
    _swiglu_half_step(x, g_ref, w1_ref, w3_ref, w2_ref, o_ref)


def _ple_final_step(h_ref, p_ref, g_ref, wg_ref, wp_ref, fg_ref, o_ref):
    hn = _rms(h_ref[...], g_ref[...]).astype(BF16)
    pb = p_ref[0].astype(BF16)
    for c in range(0, wg_ref.shape[1], MM_CHUNK):
        gate = jax.nn.sigmoid(_dot(hn, wg_ref[:, c:c + MM_CHUNK]))
        o_ref[:, c:c + MM_CHUNK] = h_ref[:, c:c + MM_CHUNK] + _dot(pb, wp_ref[:, c:c + MM_CHUNK]) * gate
    o_ref[...] = _rms(o_ref[...], fg_ref[...])


def _proj_ffn_final_kernel(h_ref, a_ref, wa_ref, g_ref, w1_ref, w3_ref, w2_ref,
                           p_ref, gp_ref, wg_ref, wp_ref, fg_ref, o_ref):
    x = h_ref[...] + _dot(a_ref[...], wa_ref[...])
    _swiglu_half_step(x, g_ref, w1_ref, w3_ref, w2_ref, o_ref)
    _ple_final_step(o_ref, p_ref, gp_ref, wg_ref, wp_ref, fg_ref, o_ref)


def _resident(shape):
    return pl.BlockSpec(shape, lambda *_: (0,) * len(shape), pipeline_mode=pl.Buffered(1))


def _ffn_call(body, lead, lead_specs, g, w1, w3, w2, name, tail=(), tail_specs=()):
    n, d = lead[0].shape
    dff = w1.shape[1]
    assert dff % FFN_CHUNK == 0
    return pl.pallas_call(
        body,
        grid=(n // FFN_ROWS,),
        in_specs=list(lead_specs) + [_resident((1, d)), _resident((d, dff)), _resident((d, dff)),
                                     _resident((dff, d))] + list(tail_specs),
        out_specs=pl.BlockSpec((FFN_ROWS, d), lambda i: (i, 0)),
        out_shape=jax.ShapeDtypeStruct((n, d), F32),
        compiler_params=_params("parallel"),
        name=name,
    )(*lead, g.reshape(1, d), w1, w3, w2, *tail)


def _rows(width):
    return pl.BlockSpec((FFN_ROWS, width), lambda i: (i, 0))


def _ffn(x, g, w1, w3, w2):
    return _ffn_call(_ffn_kernel, [x], [_rows(x.shape[1])], g, w1, w3, w2, "ffn")


def _proj_ffn(h, a, wa, g, w1, w3, w2):
    return _ffn_call(_proj_ffn_kernel, [h, a, wa],
                     [_rows(h.shape[1]), _rows(a.shape[1]), _resident(wa.shape)],
                     g, w1, w3, w2, "proj_ffn")


def _proj_ffn_final(h, a, wa, g, w1, w3, w2, p, layer, gp, w_gate, w_proj, final_g):
    d = h.shape[1]
    p_spec = pl.BlockSpec((1, FFN_ROWS, p.shape[2]), lambda i: (layer, i, 0))
    return _ffn_call(_proj_ffn_final_kernel, [h, a, wa],
                     [_rows(d), _rows(a.shape[1]), _resident(wa.shape)],
                     g, w1, w3, w2, "proj_ffn_final",
                     tail=[p, gp.reshape(1, d), w_gate, w_proj, final_g.reshape(1, d)],
                     tail_specs=[p_spec, _resident((1, d)), _resident(w_gate.shape),
                                 _resident(w_proj.shape), _resident((1, d))])


def _ple_ffn(h, p, layer, gp, w_gate, w_proj, g, w1, w3, w2):
    d = h.shape[1]
    p_spec = pl.BlockSpec((1, FFN_ROWS, p.shape[2]), lambda i: (layer, i, 0))
    return _ffn_call(_ple_ffn_kernel, [h, p, gp.reshape(1, d), w_gate, w_proj],
                     [_rows(d), p_spec, _resident((1, d)), _resident(w_gate.shape),
                      _resident(w_proj.shape)],
                     g, w1, w3, w2, "ple_ffn")


def _norm_mm_kernel(x_ref, g_ref, w_ref, o_ref):
    xn = _rms(x_ref[...], g_ref[...]).astype(BF16)
    for c in range(0, w_ref.shape[1], MM_CHUNK):
        o_ref[:, c:c + MM_CHUNK] = _dot(xn, w_ref[:, c:c + MM_CHUNK]).astype(o_ref.dtype)


def _norm_mm(x, g, w):
    n, d = x.shape
    dout = w.shape[1]
    tm = min(512, n)
    assert dout % MM_CHUNK == 0
    return pl.pallas_call(
        _norm_mm_kernel,
        grid=(n // tm,),
        in_specs=[
            pl.BlockSpec((tm, d), lambda i: (i, 0)),
            _resident((1, d)),
            _resident((d, dout)),
        ],
        out_specs=pl.BlockSpec((tm, dout), lambda i: (i, 0)),
        out_shape=jax.ShapeDtypeStruct((n, dout), BF16),
        compiler_params=_params("parallel"),
        name="norm_mm",
    )(x, g.reshape(1, d), w)


def _lane_permutation():
    a = jnp.arange(S5_T_LO * LANES)
    t_lo, g8, h = a // LANES, (a // S5_GROUP) % S5_T_LO, a % S5_GROUP
    b = g8 * LANES + t_lo * S5_GROUP + h
    return (b[:, None] == a[None, :]).astype(BF16)


def _frame_rows(j, t_hi, t_lo, rb):
    return pl.ds(j * rb * S5_PITCH + t_hi * S5_T_LO + t_lo, rb, stride=S5_PITCH)


def _chunk_rows(j, r, rb):
    return pl.ds((j * rb + r) * S5_PITCH, S5_CHUNK)


def _s5_in_kernel(x_ref, g_ref, w_ref, perm_ref, z_ref, xs_ref):
    rb = z_ref.shape[2]
    nj = x_ref.shape[1] // LANES
    blk = S5_T_HI * rb
    for j in range(nj):
        for r in range(rb):
            xs_ref[_chunk_rows(j, r, rb), :] = x_ref[r * S5_CHUNK:(r + 1) * S5_CHUNK,
                                                     j * LANES:(j + 1) * LANES]
    xperm = jnp.concatenate(
        [jnp.concatenate([xs_ref[_frame_rows(j, th, tl, rb), :] for j in range(nj)], axis=1)
         for tl in range(S5_T_LO) for th in range(S5_T_HI)], axis=0)
    u = _dot(_rms(xperm, g_ref[...]).astype(BF16), w_ref[...]).astype(BF16)
    lhs = jnp.concatenate(
        [jnp.concatenate([u[tl * blk:(tl + 1) * blk, j * LANES:(j + 1) * LANES]
                          for tl in range(S5_T_LO)], axis=1) for j in range(nj)], axis=0)
    zall = _dot(lhs, perm_ref[...]).astype(BF16)
    for j in range(nj):
        for th in range(S5_T_HI):
            r0 = (j * S5_T_HI + th) * rb
            for g8 in range(S5_T_LO):
                z_ref[j * S5_T_LO + g8, th, :, :] = zall[r0:r0 + rb, g8 * LANES:(g8 + 1) * LANES]


def _s5_in(x, g, w, perm, b, l):
    n, d = x.shape
    rb = S5_ROWS
    tm = rb * S5_CHUNK
    groups = d // S5_GROUP
    chunks = n // S5_CHUNK
    assert n % tm == 0 and l % S5_CHUNK == 0 and d % LANES == 0
    return pl.pallas_call(
        _s5_in_kernel,
        grid=(n // tm,),
        in_specs=[
            pl.BlockSpec((tm, d), lambda i: (i, 0)),
            _resident((1, d)),
            _resident(w.shape),
            _resident(perm.shape),
        ],
        out_specs=pl.BlockSpec((groups, S5_T_HI, rb, LANES), lambda i: (0, 0, i, 0)),
        out_shape=jax.ShapeDtypeStruct((groups, S5_T_HI, chunks, LANES), BF16),
        scratch_shapes=[pltpu.VMEM((d // LANES * rb * S5_PITCH, LANES), F32)],
        compiler_params=_params("parallel"),
        name="s5_in",
    )(x, g.reshape(1, d), w, perm)


def _s5_out_kernel(h_ref, z_ref, w_ref, permt_ref, o_ref, us_ref):
    rb = z_ref.shape[2]
    tm, d = h_ref.shape
    nj = d // LANES
    blk = S5_T_HI * rb
    lhs = jnp.concatenate(
        [jnp.concatenate([z_ref[j * S5_T_LO + g8, th] for g8 in range(S5_T_LO)], axis=1)
         for j in range(nj) for th in range(S5_T_HI)], axis=0)
    zp = _dot(lhs, permt_ref[...]).astype(BF16)
    z = jnp.concatenate(
        [jnp.concatenate([zp[j * blk:(j + 1) * blk, tl * LANES:(tl + 1) * LANES]
                          for j in range(nj)], axis=1) for tl in range(S5_T_LO)], axis=0)
    for c in range(0, d, MM_CHUNK):
        out = _dot(z, w_ref[:, c:c + MM_CHUNK])
        gate = _dot(z, w_ref[:, d + c:d + c + MM_CHUNK])
        upd = out * jax.nn.sigmoid(gate)
        for tl in range(S5_T_LO):
            for th in range(S5_T_HI):
                r0 = (tl * S5_T_HI + th) * rb
                for jj in range(MM_CHUNK // LANES):
                    us_ref[_frame_rows(c // LANES + jj, th, tl, rb), :] = (
                        upd[r0:r0 + rb, jj * LANES:(jj + 1) * LANES])
    for j in range(nj):
        for r in range(rb):
            rows = slice(r * S5_CHUNK, (r + 1) * S5_CHUNK)
            lanes = slice(j * LANES, (j + 1) * LANES)
            o_ref[rows, lanes] = h_ref[rows, lanes] + us_ref[_chunk_rows(j, r, rb), :]


def _s5_out(h, zg, w, permt):
    n, d = h.shape
    rb = S5_ROWS
    tm = rb * S5_CHUNK
    return pl.pallas_call(
        _s5_out_kernel,
        grid=(n // tm,),
        in_specs=[
            pl.BlockSpec((tm, d), lambda i: (i, 0)),
            pl.BlockSpec((zg.shape[0], S5_T_HI, rb, LANES), lambda i: (0, 0, i, 0)),
            _resident(w.shape),
            _resident(permt.shape),
        ],
        out_specs=pl.BlockSpec((tm, d), lambda i: (i, 0)),
        out_shape=jax.ShapeDtypeStruct((n, d), F32),
        scratch_shapes=[pltpu.VMEM((d // LANES * rb * S5_PITCH, LANES), F32)],
        compiler_params=_params("parallel"),
        name="s5_out",
    )(h, zg, w, permt)


def _ple_final(h, p, layer, g, w_gate, w_proj, final_g):
    n, d = h.shape
    tm = min(512, n)
    return pl.pallas_call(
        _ple_final_step,
        grid=(n // tm,),
        in_specs=[
            pl.BlockSpec((tm, d), lambda i: (i, 0)),
            pl.BlockSpec((1, tm, p.shape[2]), lambda i: (layer, i, 0)),
            _resident((1, d)),
            _resident(w_gate.shape),
            _resident(w_proj.shape),
            _resident((1, d)),
        ],
        out_specs=pl.BlockSpec((tm, d), lambda i: (i, 0)),
        out_shape=jax.ShapeDtypeStruct((n, d), F32),
        compiler_params=_params("parallel"),
        name="ple_final",
    )(h, p, g.reshape(1, d), w_gate, w_proj, final_g.reshape(1, d))


def _s5_kernel(x_ref, k_ref, bc_ref, pw_ref, dec_ref, d_ref, z_ref,
               m_ref, bm_ref, cm_ref, sin_ref, sw_ref, sp_ref, *, nb, nc):
    hh = k_ref.shape[1]
    t = m_ref.shape[0] // hh
    ns = sp_ref.shape[1] // 2

    taps = k_ref[0]
    lane = lax.broadcasted_iota(jnp.int32, taps.shape, 1)
    m_ref[0:hh, :] = taps.astype(BF16)
    for s in range(1, t):
        shifted = jnp.where(lane >= s * hh, pltpu.roll(taps, s * hh, axis=1), 0.0)
        m_ref[s * hh:(s + 1) * hh, :] = shifted.astype(BF16)

    b1, b2, c1, c2 = bc_ref[0, 0], bc_ref[0, 1], bc_ref[0, 2], bc_ref[0, 3]
    for s in range(t):
        rows = slice(s * hh, (s + 1) * hh)
        bm_ref[rows, :] = (b1 * pw_ref[0, 0, s:s + 1, :] + b2 * pw_ref[0, 1, s:s + 1, :]).astype(BF16)
        cm_ref[rows, :] = (c1 * pw_ref[0, 2, s:s + 1, :] + c2 * pw_ref[0, 3, s:s + 1, :]).astype(BF16)

    x = jnp.concatenate([x_ref[0, th] for th in range(S5_T_HI)], axis=1)
    s_in = _dot(x, bm_ref[...])
    s_in_sw = pltpu.roll(s_in, ns, axis=1)
    pitch = sp_ref.shape[0] // nb
    for b in range(nb):
        sin_ref[b * pitch:b * pitch + nc, :] = s_in[b * nc:(b + 1) * nc]
        sw_ref[b * pitch:b * pitch + nc, :] = s_in_sw[b * nc:(b + 1) * nc]
    dec = dec_ref[0]
    a_rr, a_s, a_sw = dec[0:1], dec[1:2], dec[2:3]
    st = jnp.zeros((nb, 2 * ns), F32)
    sw = jnp.zeros((nb, 2 * ns), F32)
    for c in range(nc):
        rows = pl.ds(c, nb, stride=pitch)
        sp_ref[rows, :] = st
        st, sw = (st * a_rr + sw * a_s + sin_ref[rows, :],
                  sw * a_rr + st * a_sw + sw_ref[rows, :])
    s_prev = jnp.concatenate([sp_ref[b * pitch:b * pitch + nc, :] for b in range(nb)], axis=0)

    y = _dot(x, m_ref[...]) + lax.dot_general(
        s_prev.astype(BF16), cm_ref[...], (((1,), (1,)), ((), ())), preferred_element_type=F32)
    y = y + d_ref[0] * x.astype(F32)
    z = jax.nn.gelu(y).astype(BF16)
    for th in range(S5_T_HI):
        z_ref[0, th] = z[:, th * LANES:(th + 1) * LANES]


def _s5_weights(a_re, a_im, log_dt, b_re, b_im, c_re, c_im, d_skip):
    t = S5_CHUNK
    g, ns, hh = b_re.shape
    hp = lax.Precision.HIGHEST
    lam_re = jnp.minimum(a_re.astype(F32), -1e-4)
    lam_im = a_im.astype(F32)
    dt = jnp.exp(log_dt.astype(F32))[:, None]
    mag = jnp.exp(lam_re * dt)
    abar_re = mag * jnp.cos(lam_im * dt)
    abar_im = mag * jnp.sin(lam_im * dt)
    den = lam_re * lam_re + lam_im * lam_im
    nr = abar_re - 1.0
    ni = abar_im
    fr = (nr * lam_re + ni * lam_im) / den
    fi = (ni * lam_re - nr * lam_im) / den
    bre = b_re.astype(F32)
    bim = b_im.astype(F32)
    bbar_re = fr[..., None] * bre - fi[..., None] * bim
    bbar_im = fr[..., None] * bim + fi[..., None] * bre
    cre = c_re.astype(F32)
    cim = c_im.astype(F32)

    k = jnp.arange(t + 1, dtype=F32)[None, :, None]
    pmag = jnp.exp(k * (lam_re * dt)[:, None, :])
    pw_re = pmag * jnp.cos(k * (lam_im * dt)[:, None, :])
    pw_im = pmag * jnp.sin(k * (lam_im * dt)[:, None, :])
    bt_re = bbar_re.transpose(0, 2, 1)
    bt_im = bbar_im.transpose(0, 2, 1)

    def c_times_power(w_re, w_im):
        return (cre[:, None] * w_re[:, :, None, :] - cim[:, None] * w_im[:, :, None, :],
                cre[:, None] * w_im[:, :, None, :] + cim[:, None] * w_re[:, :, None, :])

    cw_re, cw_im = c_times_power(pw_re[:, :t], pw_im[:, :t])
    cw_re = cw_re.reshape(g, t * hh, ns)
    cw_im = cw_im.reshape(g, t * hh, ns)
    taps = (jnp.einsum('gip,gnp->gin', bt_re, cw_re, precision=hp)
            - jnp.einsum('gip,gnp->gin', bt_im, cw_im, precision=hp))

    pack = lambda a, b_: jnp.concatenate([a, b_], axis=-1)
    bc = jnp.stack([pack(bt_re, bt_im), pack(-bt_im, bt_re),
                    pack(cre, -cim), pack(-cim, -cre)], axis=1)
    rev_re = pw_re[:, :t][:, ::-1]
    rev_im = pw_im[:, :t][:, ::-1]
    pw4 = jnp.stack([pack(rev_re, rev_re), pack(rev_im, rev_im),
                     pack(pw_re[:, 1:], pw_re[:, 1:]), pack(pw_im[:, 1:], pw_im[:, 1:])],
                    axis=1)

    at_re, at_im = pw_re[:, t], pw_im[:, t]
    dec = jnp.stack([
        jnp.concatenate([at_re, at_re], axis=1),
        jnp.concatenate([-at_im, at_im], axis=1),
        jnp.concatenate([at_im, -at_im], axis=1),
    ], axis=1)
    dec = jnp.pad(dec, ((0, 0), (0, 5), (0, 0)))

    dflat = jnp.tile(d_skip.astype(F32).reshape(g, 1, hh), (1, t, 1)).reshape(g, 1, t * hh)
    return taps, bc, pw4, dec, dflat


def _s5_core(zg, weights, b, l):
    taps, bc, pw4, dec, dflat = weights
    g, hh = taps.shape[0], taps.shape[1]
    ns2 = bc.shape[3]
    nc = l // S5_CHUNK
    rows = nc * b
    th = S5_CHUNK * hh
    blk = (1, S5_T_HI, rows, LANES)
    return pl.pallas_call(
        functools.partial(_s5_kernel, nb=b, nc=nc),
        grid=(g,),
        in_specs=[
            pl.BlockSpec(blk, lambda i: (i, 0, 0, 0)),
            pl.BlockSpec((1, hh, th), lambda i: (i, 0, 0)),
            pl.BlockSpec((1,) + bc.shape[1:], lambda i: (i, 0, 0, 0)),
            pl.BlockSpec((1,) + pw4.shape[1:], lambda i: (i, 0, 0, 0)),
            pl.BlockSpec((1, 8, ns2), lambda i: (i, 0, 0)),
            pl.BlockSpec((1, 1, th), lambda i: (i, 0, 0)),
        ],
        out_specs=pl.BlockSpec(blk, lambda i: (i, 0, 0, 0)),
        out_shape=jax.ShapeDtypeStruct(zg.shape, BF16),
        scratch_shapes=([pltpu.VMEM((th, th), BF16)] + [pltpu.VMEM((th, ns2), BF16)] * 2
                        + [pltpu.VMEM((b * (nc + 8), ns2), F32)] * 3),
        compiler_params=_params("parallel"),
        name="s5_core",
    )(zg, taps, bc, pw4, dec, dflat)


def _sb_kernel(q_ref, k_ref, v_ref, o_ref, carry_ref, acc_ref):
    tk = SB_TK
    tq = carry_ref.shape[1]
    nsub = tq // tk
    nq = q_ref.shape[1] // tq
    two = 2 * LANES

    lane = lax.broadcasted_iota(jnp.int32, (tk, two), 1)
    row = lax.broadcasted_iota(jnp.int32, (tk, two), 0)
    strict = (lane & (LANES - 1)) < row
    trow = lax.broadcasted_iota(jnp.int32, (two, two), 0)
    tcol = lax.broadcasted_iota(jnp.int32, (two, two), 1)
    tri = jnp.where((tcol >= LANES) | ((trow & (LANES - 1)) > tcol), -1.0, 0.0).astype(BF16)
    head0 = lax.broadcasted_iota(jnp.int32, (tk, LANES), 1) < SB_HEAD_DIM

    def split_heads(t):
        zero = jnp.zeros_like(t)
        return jnp.concatenate([jnp.where(head0, t, zero), jnp.where(head0, zero, t)], axis=0)

    def mask_diag(a):
        top = jnp.where(strict, a[:tk], 0.0)
        return top if a.shape[0] == tk else jnp.concatenate([top, a[tk:]], axis=0)

    def qk(q_rows, kb):
        return lax.dot_general(q_rows, split_heads(kb), (((1,), (1,)), ((), ())),
                               preferred_element_type=F32)

    def logs(z, diag):
        nlk = jnp.maximum(z, 0.0) + jnp.log2(1.0 + jnp.exp2(-jnp.abs(z)))
        ls = z - nlk
        if diag:
            nlk = mask_diag(nlk)
        hi = nlk.astype(BF16)
        lo = (nlk - hi.astype(F32)).astype(BF16)
        r = [_dot(jnp.concatenate([hi[:, h * LANES:(h + 1) * LANES],
                                   lo[:, h * LANES:(h + 1) * LANES]], axis=1), tri)
             for h in range(2)]
        return ls, r

    def attend(ls, r, carry, diag):
        arg = jnp.concatenate([r[h][:, :LANES] + carry[h] for h in range(2)], axis=1) + ls
        att = jnp.exp2(arg)
        if diag:
            att = mask_diag(att)
        return att.astype(BF16), [carry[h] + r[h][:, LANES:] for h in range(2)]

    def q_tile(i, first):
        q0 = 0 if first else pl.multiple_of(i * tq, tq)
        q = q_ref[0, pl.ds(q0, tq), :]
        carry_ref[...] = jnp.zeros_like(carry_ref)
        acc_ref[...] = jnp.zeros_like(acc_ref)

        def sweep(tiles):
            zs = [qk(q[r0:], k_ref[0, pl.ds(k0, tk), :]) for r0, k0, _ in tiles]
            lrs = [logs(z, diag) for z, (_, _, diag) in zip(zs, tiles)]
            for (ls, r), (r0, k0, diag) in zip(lrs, tiles):
                carry = [carry_ref[h, r0:, :] for h in range(2)]
                att, carry = attend(ls, r, carry, diag)
                for h in range(2):
                    carry_ref[h, r0:, :] = carry[h]
                acc_ref[r0:, :] += _dot(att, split_heads(v_ref[0, pl.ds(k0, tk), :]))
            return jnp.max(jnp.maximum(carry[0], carry[1]))

        def pair(jj):
            return [(0, pl.multiple_of(q0 - (2 * jj + 1) * tk, tk), False),
                    (0, pl.multiple_of(q0 - (2 * jj + 2) * tk, tk), False)]

        diag_tiles = [(kk * tk, kk * tk if first else pl.multiple_of(q0 + kk * tk, tk), True)
                      for kk in reversed(range(nsub))]
        if first:
            sweep(diag_tiles)
        else:
            def more_keys(state):
                jj, carry_max = state
                return (jj < i * (nsub // 2)) & (carry_max > SB_LOG2_ZERO)

            lax.while_loop(more_keys, lambda state: (state[0] + 1, sweep(pair(state[0]))),
                           (jnp.int32(1), sweep(diag_tiles + pair(0))))
        o_ref[0, pl.ds(q0, tq), :] = acc_ref[...].astype(o_ref.dtype)

    q_tile(0, True)

    def later_q_tile(i, _):
        q_tile(i, False)
        return 0

    lax.fori_loop(1, nq, later_q_tile, 0)


def _sb_attention(qkv, d):
    b, l, _ = qkv.shape
    nhp = d // LANES
    tq = min(SB_TQ, l)
    assert tq % (2 * SB_TK) == 0 and l % tq == 0
    return pl.pallas_call(
        _sb_kernel,
        grid=(b, nhp),
        in_specs=[
            pl.BlockSpec((1, l, LANES), lambda i, j: (i, 0, j)),
            pl.BlockSpec((1, l, LANES), lambda i, j: (i, 0, nhp + j)),
            pl.BlockSpec((1, l, LANES), lambda i, j: (i, 0, 2 * nhp + j)),
        ],
        out_specs=pl.BlockSpec((1, l, LANES), lambda i, j: (i, 0, j)),
        out_shape=jax.ShapeDtypeStruct((b, l, d), BF16),
        scratch_shapes=[pltpu.VMEM((2, tq, LANES), F32), pltpu.VMEM((tq, LANES), F32)],
        compiler_params=_params("parallel", "parallel"),
        name="sb_attention",
    )(qkv, qkv, qkv)


def kernel(x, p, ffn1_norm, ffn1_w1, ffn1_w3, ffn1_w2, mix_norm, ffn2_norm, ffn2_w1, ffn2_w3, ffn2_w2, ple_norm, ple_proj, ple_gate, s5_w_in, s5_a_re, s5_a_im, s5_log_dt, s5_b_re, s5_b_im, s5_c_re, s5_c_im, s5_d, s5_w_glu, sb_w_qkv, sb_w_o, final_norm):
    b, l, d = x.shape
    depth = p.shape[0]
    n = b * l
    bf = lambda w: w.astype(BF16)
    f1w1, f1w3, f1w2 = bf(ffn1_w1), bf(ffn1_w3), bf(ffn1_w2)
    f2w1, f2w3, f2w2 = bf(ffn2_w1), bf(ffn2_w3), bf(ffn2_w2)
    w_gate, w_proj = bf(ple_gate), bf(ple_proj)
    p = p.reshape(depth, n, p.shape[-1])
    last = depth - 1
    h = x.reshape(n, d)
    for i in range(depth):
        if i == 0:
            h = _ffn(h, ffn1_norm[i], f1w1[i], f1w3[i], f1w2[i])
        else:
            h = _ple_ffn(h, p, i - 1, ple_norm[i - 1], w_gate[i - 1], w_proj[i - 1],
                         ffn1_norm[i], f1w1[i], f1w3[i], f1w2[i])
        j = i // 2
        if i % 2 == 0:
            perm = _lane_permutation()
            zg = _s5_in(h, mix_norm[i], bf(s5_w_in[j]), perm, b, l)
            weights = _s5_weights(s5_a_re[j], s5_a_im[j], s5_log_dt[j], s5_b_re[j], s5_b_im[j],
                                  s5_c_re[j], s5_c_im[j], s5_d[j])
            h = _s5_out(h, _s5_core(zg, weights, b, l), bf(s5_w_glu[j]), perm.T)
            h = _ffn(h, ffn2_norm[i], f2w1[i], f2w3[i], f2w2[i])
            if i == last:
                h = _ple_final(h, p, i, ple_norm[i], w_gate[i], w_proj[i], final_norm)
        else:
            col_scale = jnp.where(jnp.arange(3 * d) < d, LOG2_E * SB_HEAD_DIM ** -0.5, 1.0).astype(F32)
            qkv = _norm_mm(h, mix_norm[i], bf(sb_w_qkv[j] * col_scale))
            o = _sb_attention(qkv.reshape(b, l, 3 * d), d)
            ffn2 = (ffn2_norm[i], f2w1[i], f2w3[i], f2w2[i])
            if i == last:
                h = _proj_ffn_final(h, o.reshape(n, d), bf(sb_w_o[j]), *ffn2,
                                    p, i, ple_norm[i], w_gate[i], w_proj[i], final_norm)
            else:
                h = _proj_ffn(h, o.reshape(n, d), bf(sb_w_o[j]), *ffn2)
    return h.reshape(b, l, d)
```

```python
import functools

import jax
import jax.numpy as jnp
from jax import lax
from jax.experimental import pallas as pl
from jax.experimental.pallas import tpu as pltpu

F32 = jnp.float32
BF16 = jnp.bfloat16

EPS = 1e-6
S5_GROUP = 16
S5_STATE = 64
S5_CHUNK = 64
SB_HEAD_DIM = 64
LANES = 128
S5_T_LO = LANES // S5_GROUP
S5_T_HI = S5_CHUNK // S5_T_LO
S5_ROWS = 16
S5_PITCH = S5_CHUNK + 8
SB_TQ = 512
SB_TK = 128
SB_LOG2_ZERO = -150.0
LOG2_E = 1.4426950408889634
FFN_CHUNK = 256
FFN_ROWS = 512
MM_CHUNK = 512
VMEM_LIMIT_BYTES = 56 * 1024 * 1024


def _params(*semantics):
    return pltpu.CompilerParams(dimension_semantics=semantics,
                                vmem_limit_bytes=VMEM_LIMIT_BYTES)


def _rms(x, g):
    return x * lax.rsqrt(jnp.mean(x * x, axis=-1, keepdims=True) + EPS) * g


def _dot(a, b):
    return jnp.dot(a, b, preferred_element_type=F32)


def _swiglu_half_step(x, g_ref, w1_ref, w3_ref, w2_ref, o_ref):
    xn = _rms(x, g_ref[...]).astype(BF16)
    o_ref[...] = x
    dff = w1_ref.shape[1]
    for c in range(0, dff, FFN_CHUNK):
        a = jax.nn.silu(_dot(xn, w1_ref[:, c:c + FFN_CHUNK])) * _dot(xn, w3_ref[:, c:c + FFN_CHUNK])
        o_ref[...] += _dot((0.5 * a).astype(BF16), w2_ref[c:c + FFN_CHUNK, :])


def _ffn_kernel(x_ref, g_ref, w1_ref, w3_ref, w2_ref, o_ref):
    _swiglu_half_step(x_ref[...], g_ref, w1_ref, w3_ref, w2_ref, o_ref)


def _proj_ffn_kernel(h_ref, a_ref, wa_ref, g_ref, w1_ref, w3_ref, w2_ref, o_ref):
    x = h_ref[...] + _dot(a_ref[...], wa_ref[...])
    _swiglu_half_step(x, g_ref, w1_ref, w3_ref, w2_ref, o_ref)


def _ple_ffn_kernel(h_ref, p_ref, gp_ref, wg_ref, wp_ref, g_ref, w1_ref, w3_ref, w2_ref, o_ref):
    h = h_ref[...]
    gate = jax.nn.sigmoid(_dot(_rms(h, gp_ref[...]).astype(BF16), wg_ref[...]))
    x = h + _dot(p_ref[0].astype(BF16), wp_ref[...]) * gate
    _swiglu_half_step(x, g_ref, w1_ref, w3_ref, w2_ref, o_ref)


def _ple_final_step(h_ref, p_ref, g_ref, wg_ref, wp_ref, fg_ref, o_ref):
    hn = _rms(h_ref[...], g_ref[...]).astype(BF16)
    pb = p_ref[0].astype(BF16)
    for c in range(0, wg_ref.shape[1], MM_CHUNK):
        gate = jax.nn.sigmoid(_dot(hn, wg_ref[:, c:c + MM_CHUNK]))
        o_ref[:, c:c + MM_CHUNK] = h_ref[:, c:c + MM_CHUNK] + _dot(pb, wp_ref[:, c:c + MM_CHUNK]) * gate
    o_ref[...] = _rms(o_ref[...], fg_ref[...])


def _proj_ffn_final_kernel(h_ref, a_ref, wa_ref, g_ref, w1_ref, w3_ref, w2_ref,
                           p_ref, gp_ref, wg_ref, wp_ref, fg_ref, o_ref):
    x = h_ref[...] + _dot(a_ref[...], wa_ref[...])
    _swiglu_half_step(x, g_ref, w1_ref, w3_ref, w2_ref, o_ref)
    _ple_final_step(o_ref, p_ref, gp_ref, wg_ref, wp_ref, fg_ref, o_ref)


def _resident(shape):
    return pl.BlockSpec(shape, lambda *_: (0,) * len(shape), pipeline_mode=pl.Buffered(1))


def _ffn_call(body, lead, lead_specs, g, w1, w3, w2, name, tail=(), tail_specs=()):
    n, d = lead[0].shape
    dff = w1.shape[1]
    assert dff % FFN_CHUNK == 0
    return pl.pallas_call(
        body,
        grid=(n // FFN_ROWS,),
        in_specs=list(lead_specs) + [_resident((1, d)), _resident((d, dff)), _resident((d, dff)),
                                     _resident((dff, d))] + list(tail_specs),
        out_specs=pl.BlockSpec((FFN_ROWS, d), lambda i: (i, 0)),
        out_shape=jax.ShapeDtypeStruct((n, d), F32),
        compiler_params=_params("parallel"),
        name=name,
    )(*lead, g.reshape(1, d), w1, w3, w2, *tail)


def _rows(width):
    return pl.BlockSpec((FFN_ROWS, width), lambda i: (i, 0))


def _ffn(x, g, w1, w3, w2):
    return _ffn_call(_ffn_kernel, [x], [_rows(x.shape[1])], g, w1, w3, w2, "ffn")


def _proj_ffn(h, a, wa, g, w1, w3, w2):
    return _ffn_call(_proj_ffn_kernel, [h, a, wa],
                     [_rows(h.shape[1]), _rows(a.shape[1]), _resident(wa.shape)],
                     g, w1, w3, w2, "proj_ffn")


def _proj_ffn_final(h, a, wa, g, w1, w3, w2, p, layer, gp, w_gate, w_proj, final_g):
    d = h.shape[1]
    p_spec = pl.BlockSpec((1, FFN_ROWS, p.shape[2]), lambda i: (layer, i, 0))
    return _ffn_call(_proj_ffn_final_kernel, [h, a, wa],
                     [_rows(d), _rows(a.shape[1]), _resident(wa.shape)],
                     g, w1, w3, w2, "proj_ffn_final",
                     tail=[p, gp.reshape(1, d), w_gate, w_proj, final_g.reshape(1, d)],
                     tail_specs=[p_spec, _resident((1, d)), _resident(w_gate.shape),
                                 _resident(w_proj.shape), _resident((1, d))])


def _ple_ffn(h, p, layer, gp, w_gate, w_proj, g, w1, w3, w2):
    d = h.shape[1]
    p_spec = pl.BlockSpec((1, FFN_ROWS, p.shape[2]), lambda i: (layer, i, 0))
    return _ffn_call(_ple_ffn_kernel, [h, p, gp.reshape(1, d), w_gate, w_proj],
                     [_rows(d), p_spec, _resident((1, d)), _resident(w_gate.shape),
                      _resident(w_proj.shape)],
                     g, w1, w3, w2, "ple_ffn")


def _norm_mm_kernel(x_ref, g_ref, w_ref, o_ref):
    xn = _rms(x_ref[...], g_ref[...]).astype(BF16)
    for c in range(0, w_ref.shape[1], MM_CHUNK):
        o_ref[:, c:c + MM_CHUNK] = _dot(xn, w_ref[:, c:c + MM_CHUNK]).astype(o_ref.dtype)


def _norm_mm(x, g, w):
    n, d = x.shape
    dout = w.shape[1]
    tm = min(512, n)
    assert dout % MM_CHUNK == 0
    return pl.pallas_call(
        _norm_mm_kernel,
        grid=(n // tm,),
        in_specs=[
            pl.BlockSpec((tm, d), lambda i: (i, 0)),
            _resident((1, d)),
            _resident((d, dout)),
        ],
        out_specs=pl.BlockSpec((tm, dout), lambda i: (i, 0)),
        out_shape=jax.ShapeDtypeStruct((n, dout), BF16),
        compiler_params=_params("parallel"),
        name="norm_mm",
    )(x, g.reshape(1, d), w)


def _lane_permutation():
    a = jnp.arange(S5_T_LO * LANES)
    t_lo, g8, h = a // LANES, (a // S5_GROUP) % S5_T_LO, a % S5_GROUP
    b = g8 * LANES + t_lo * S5_GROUP + h
    return (b[:, None] == a[None, :]).astype(BF16)


def _frame_rows(j, t_hi, t_lo, rb):
    return pl.ds(j * rb * S5_PITCH + t_hi * S5_T_LO + t_lo, rb, stride=S5_PITCH)


def _chunk_rows(j, r, rb):
    return pl.ds((j * rb + r) * S5_PITCH, S5_CHUNK)


def _s5_in_kernel(x_ref, g_ref, w_ref, perm_ref, z_ref, xs_ref):
    rb = z_ref.shape[2]
    nj = x_ref.shape[1] // LANES
    blk = S5_T_HI * rb
    for j in range(nj):
        for r in range(rb):
            xs_ref[_chunk_rows(j, r, rb), :] = x_ref[r * S5_CHUNK:(r + 1) * S5_CHUNK,
                                                     j * LANES:(j + 1) * LANES]
    xperm = jnp.concatenate(
        [jnp.concatenate([xs_ref[_frame_rows(j, th, tl, rb), :] for j in range(nj)], axis=1)
         for tl in range(S5_T_LO) for th in range(S5_T_HI)], axis=0)
    u = _dot(_rms(xperm, g_ref[...]).astype(BF16), w_ref[...]).astype(BF16)
    lhs = jnp.concatenate(
        [jnp.concatenate([u[tl * blk:(tl + 1) * blk, j * LANES:(j + 1) * LANES]
                          for tl in range(S5_T_LO)], axis=1) for j in range(nj)], axis=0)
    zall = _dot(lhs, perm_ref[...]).astype(BF16)
    for j in range(nj):
        for th in range(S5_T_HI):
            r0 = (j * S5_T_HI + th) * rb
            for g8 in range(S5_T_LO):
                z_ref[j * S5_T_LO + g8, th, :, :] = zall[r0:r0 + rb, g8 * LANES:(g8 + 1) * LANES]


def _s5_in(x, g, w, perm, b, l):
    n, d = x.shape
    rb = S5_ROWS
    tm = rb * S5_CHUNK
    groups = d // S5_GROUP
    chunks = n // S5_CHUNK
    assert n % tm == 0 and l % S5_CHUNK == 0 and d % LANES == 0
    return pl.pallas_call(
        _s5_in_kernel,
        grid=(n // tm,),
        in_specs=[
            pl.BlockSpec((tm, d), lambda i: (i, 0)),
            _resident((1, d)),
            _resident(w.shape),
            _resident(perm.shape),
        ],
        out_specs=pl.BlockSpec((groups, S5_T_HI, rb, LANES), lambda i: (0, 0, i, 0)),
        out_shape=jax.ShapeDtypeStruct((groups, S5_T_HI, chunks, LANES), BF16),
        scratch_shapes=[pltpu.VMEM((d // LANES * rb * S5_PITCH, LANES), F32)],
        compiler_params=_params("parallel"),
        name="s5_in",
    )(x, g.reshape(1, d), w, perm)


def _s5_out_kernel(h_ref, z_ref, w_ref, permt_ref, o_ref, us_ref):
    rb = z_ref.shape[2]
    tm, d = h_ref.shape
    nj = d // LANES
    blk = S5_T_HI * rb
    lhs = jnp.concatenate(
        [jnp.concatenate([z_ref[j * S5_T_LO + g8, th] for g8 in range(S5_T_LO)], axis=1)
         for j in range(nj) for th in range(S5_T_HI)], axis=0)
    zp = _dot(lhs, permt_ref[...]).astype(BF16)
    z = jnp.concatenate(
        [jnp.concatenate([zp[j * blk:(j + 1) * blk, tl * LANES:(tl + 1) * LANES]
                          for j in range(nj)], axis=1) for tl in range(S5_T_LO)], axis=0)
    for c in range(0, d, MM_CHUNK):
        out = _dot(z, w_ref[:, c:c + MM_CHUNK])
        gate = _dot(z, w_ref[:, d + c:d + c + MM_CHUNK])
        upd = out * jax.nn.sigmoid(gate)
        for tl in range(S5_T_LO):
            for th in range(S5_T_HI):
                r0 = (tl * S5_T_HI + th) * rb
                for jj in range(MM_CHUNK // LANES):
                    us_ref[_frame_rows(c // LANES + jj, th, tl, rb), :] = (
                        upd[r0:r0 + rb, jj * LANES:(jj + 1) * LANES])
    for j in range(nj):
        for r in range(rb):
            rows = slice(r * S5_CHUNK, (r + 1) * S5_CHUNK)
            lanes = slice(j * LANES, (j + 1) * LANES)
            o_ref[rows, lanes] = h_ref[rows, lanes] + us_ref[_chunk_rows(j, r, rb), :]


def _s5_out(h, zg, w, permt):
    n, d = h.shape
    rb = S5_ROWS
    tm = rb * S5_CHUNK
    return pl.pallas_call(
        _s5_out_kernel,
        grid=(n // tm,),
        in_specs=[
            pl.BlockSpec((tm, d), lambda i: (i, 0)),
            pl.BlockSpec((zg.shape[0], S5_T_HI, rb, LANES), lambda i: (0, 0, i, 0)),
            _resident(w.shape),
            _resident(permt.shape),
        ],
        out_specs=pl.BlockSpec((tm, d), lambda i: (i, 0)),
        out_shape=jax.ShapeDtypeStruct((n, d), F32),
        scratch_shapes=[pltpu.VMEM((d // LANES * rb * S5_PITCH, LANES), F32)],
        compiler_params=_params("parallel"),
        name="s5_out",
    )(h, zg, w, permt)


def _ple_final(h, p, layer, g, w_gate, w_proj, final_g):
    n, d = h.shape
    tm = min(512, n)
    return pl.pallas_call(
        _ple_final_step,
        grid=(n // tm,),
        in_specs=[
            pl.BlockSpec((tm, d), lambda i: (i, 0)),
            pl.BlockSpec((1, tm, p.shape[2]), lambda i: (layer, i, 0)),
            _resident((1, d)),
            _resident(w_gate.shape),
            _resident(w_proj.shape),
            _resident((1, d)),
        ],
        out_specs=pl.BlockSpec((tm, d), lambda i: (i, 0)),
        out_shape=jax.ShapeDtypeStruct((n, d), F32),
        compiler_params=_params("parallel"),
        name="ple_final",
    )(h, p, g.reshape(1, d), w_gate, w_proj, final_g.reshape(1, d))


def _s5_kernel(x_ref, bc_ref, pw_ref, dec_ref, d_ref, z_ref,
               m_ref, bm_ref, cw_ref, sin_ref, sw_ref, sp_ref, *, nb, nc):
    hh = bc_ref.shape[2]
    t = m_ref.shape[0] // hh
    ns = sp_ref.shape[1] // 2

    b1, b2, c1, c2 = bc_ref[0, 0], bc_ref[0, 1], bc_ref[0, 2], bc_ref[0, 3]
    for k in range(t + 1):
        cw_ref[k * hh:(k + 1) * hh, :] = c1 * pw_ref[0, 0, k:k + 1, :] + c2 * pw_ref[0, 1, k:k + 1, :]
    for s in range(t):
        k = t - 1 - s
        bm_ref[s * hh:(s + 1) * hh, :] = (
            b1 * pw_ref[0, 0, k:k + 1, :] + b2 * pw_ref[0, 1, k:k + 1, :]).astype(BF16)

    taps = lax.dot_general(b1, cw_ref[0:t * hh, :], (((1,), (1,)), ((), ())),
                           precision=lax.Precision.HIGHEST, preferred_element_type=F32)
    lane = lax.broadcasted_iota(jnp.int32, taps.shape, 1)
    m_ref[0:hh, :] = taps.astype(BF16)
    for s in range(1, t):
        shifted = jnp.where(lane >= s * hh, pltpu.roll(taps, s * hh, axis=1), 0.0)
        m_ref[s * hh:(s + 1) * hh, :] = shifted.astype(BF16)
    cmat = cw_ref[hh:(t + 1) * hh, :].astype(BF16)

    x = jnp.concatenate([x_ref[0, th] for th in range(S5_T_HI)], axis=1)
    s_in = _dot(x, bm_ref[...])
    s_in_sw = pltpu.roll(s_in, ns, axis=1)
    pitch = sp_ref.shape[0] // nb
    for b in range(nb):
        sin_ref[b * pitch:b * pitch + nc, :] = s_in[b * nc:(b + 1) * nc]
        sw_ref[b * pitch:b * pitch + nc, :] = s_in_sw[b * nc:(b + 1) * nc]
    dec = dec_ref[0]
    a_rr, a_s, a_sw = dec[0:1], dec[1:2], dec[2:3]
    st = jnp.zeros((nb, 2 * ns), F32)
    sw = jnp.zeros((nb, 2 * ns), F32)
    for c in range(nc):
        rows = pl.ds(c, nb, stride=pitch)
        sp_ref[rows, :] = st
        st, sw = (st * a_rr + sw * a_s + sin_ref[rows, :],
                  sw * a_rr + st * a_sw + sw_ref[rows, :])
    s_prev = jnp.concatenate([sp_ref[b * pitch:b * pitch + nc, :] for b in range(nb)], axis=0)

    y = _dot(x, m_ref[...]) + lax.dot_general(
        s_prev.astype(BF16), cmat, (((1,), (1,)), ((), ())), preferred_element_type=F32)
    y = y + d_ref[0] * x.astype(F32)
    z = jax.nn.gelu(y).astype(BF16)
    for th in range(S5_T_HI):
        z_ref[0, th] = z[:, th * LANES:(th + 1) * LANES]


def _s5_weights(a_re, a_im, log_dt, b_re, b_im, c_re, c_im, d_skip):
    t = S5_CHUNK
    g, _, hh = b_re.shape
    lam_re = jnp.minimum(a_re.astype(F32), -1e-4)
    lam_im = a_im.astype(F32)
    dt = jnp.exp(log_dt.astype(F32))[:, None]
    mag = jnp.exp(lam_re * dt)
    abar_re = mag * jnp.cos(lam_im * dt)
    abar_im = mag * jnp.sin(lam_im * dt)
    den = lam_re * lam_re + lam_im * lam_im
    nr = abar_re - 1.0
    ni = abar_im
    fr = (nr * lam_re + ni * lam_im) / den
    fi = (ni * lam_re - nr * lam_im) / den
    bre = b_re.astype(F32)
    bim = b_im.astype(F32)
    bbar_re = fr[..., None] * bre - fi[..., None] * bim
    bbar_im = fr[..., None] * bim + fi[..., None] * bre
    cre = c_re.astype(F32)
    cim = c_im.astype(F32)

    k = jnp.arange(t + 1, dtype=F32)[None, :, None]
    pmag = jnp.exp(k * (lam_re * dt)[:, None, :])
    pw_re = pmag * jnp.cos(k * (lam_im * dt)[:, None, :])
    pw_im = pmag * jnp.sin(k * (lam_im * dt)[:, None, :])
    bt_re = bbar_re.transpose(0, 2, 1)
    bt_im = bbar_im.transpose(0, 2, 1)

    pack = lambda a, b_: jnp.concatenate([a, b_], axis=-1)
    bc = jnp.stack([pack(bt_re, bt_im), pack(-bt_im, bt_re),
                    pack(cre, -cim), pack(-cim, -cre)], axis=1)
    pw2 = jnp.stack([pack(pw_re, pw_re), pack(pw_im, pw_im)], axis=1)

    at_re, at_im = pw_re[:, t], pw_im[:, t]
    dec = jnp.stack([
        jnp.concatenate([at_re, at_re], axis=1),
        jnp.concatenate([-at_im, at_im], axis=1),
        jnp.concatenate([at_im, -at_im], axis=1),
    ], axis=1)
    dec = jnp.pad(dec, ((0, 0), (0, 5), (0, 0)))

    dflat = jnp.tile(d_skip.astype(F32).reshape(g, 1, hh), (1, t, 1)).reshape(g, 1, t * hh)
    return bc, pw2, dec, dflat


def _s5_core(zg, weights, b, l):
    bc, pw2, dec, dflat = weights
    g, hh, ns2 = bc.shape[0], bc.shape[2], bc.shape[3]
    nc = l // S5_CHUNK
    rows = nc * b
    th = S5_CHUNK * hh
    blk = (1, S5_T_HI, rows, LANES)
    return pl.pallas_call(
        functools.partial(_s5_kernel, nb=b, nc=nc),
        grid=(g,),
        in_specs=[
            pl.BlockSpec(blk, lambda i: (i, 0, 0, 0)),
            pl.BlockSpec((1,) + bc.shape[1:], lambda i: (i, 0, 0, 0)),
            pl.BlockSpec((1,) + pw2.shape[1:], lambda i: (i, 0, 0, 0)),
            pl.BlockSpec((1, 8, ns2), lambda i: (i, 0, 0)),
            pl.BlockSpec((1, 1, th), lambda i: (i, 0, 0)),
        ],
        out_specs=pl.BlockSpec(blk, lambda i: (i, 0, 0, 0)),
        out_shape=jax.ShapeDtypeStruct(zg.shape, BF16),
        scratch_shapes=([pltpu.VMEM((th, th), BF16), pltpu.VMEM((th, ns2), BF16),
                         pltpu.VMEM((th + hh, ns2), F32)]
                        + [pltpu.VMEM((b * (nc + 8), ns2), F32)] * 3),
        compiler_params=_params("parallel"),
        name="s5_core",
    )(zg, bc, pw2, dec, dflat)


def _sb_kernel(q_ref, k_ref, v_ref, o_ref, carry_ref, acc_ref):
    tk = SB_TK
    tq = carry_ref.shape[1]
    nsub = tq // tk
    nq = q_ref.shape[1] // tq
    two = 2 * LANES

    lane = lax.broadcasted_iota(jnp.int32, (tk, two), 1)
    row = lax.broadcasted_iota(jnp.int32, (tk, two), 0)
    strict = (lane & (LANES - 1)) < row
    trow = lax.broadcasted_iota(jnp.int32, (two, two), 0)
    tcol = lax.broadcasted_iota(jnp.int32, (two, two), 1)
    tri = jnp.where((tcol >= LANES) | ((trow & (LANES - 1)) > tcol), -1.0, 0.0).astype(BF16)
    head0 = lax.broadcasted_iota(jnp.int32, (tk, LANES), 1) < SB_HEAD_DIM

    def split_heads(t):
        zero = jnp.zeros_like(t)
        return jnp.concatenate([jnp.where(head0, t, zero), jnp.where(head0, zero, t)], axis=0)

    def mask_diag(a):
        top = jnp.where(strict, a[:tk], 0.0)
        return top if a.shape[0] == tk else jnp.concatenate([top, a[tk:]], axis=0)

    def qk(q_rows, kb):
        return lax.dot_general(q_rows, split_heads(kb), (((1,), (1,)), ((), ())),
                               preferred_element_type=F32)

    def logs(z, diag):
        nlk = jnp.maximum(z, 0.0) + jnp.log2(1.0 + jnp.exp2(-jnp.abs(z)))
        ls = z - nlk
        if diag:
            nlk = mask_diag(nlk)
        hi = nlk.astype(BF16)
        lo = (nlk - hi.astype(F32)).astype(BF16)
        r = [_dot(jnp.concatenate([hi[:, h * LANES:(h + 1) * LANES],
                                   lo[:, h * LANES:(h + 1) * LANES]], axis=1), tri)
             for h in range(2)]
        return ls, r

    def attend(ls, r, carry, diag):
        arg = jnp.concatenate([r[h][:, :LANES] + carry[h] for h in range(2)], axis=1) + ls
        att = jnp.exp2(arg)
        if diag:
            att = mask_diag(att)
        return att.astype(BF16), [carry[h] + r[h][:, LANES:] for h in range(2)]

    def q_tile(i, first):
        q0 = 0 if first else pl.multiple_of(i * tq, tq)
        q = q_ref[0, pl.ds(q0, tq), :]
        carry_ref[...] = jnp.zeros_like(carry_ref)
        acc_ref[...] = jnp.zeros_like(acc_ref)

        def sweep(tiles):
            zs = [qk(q[r0:], k_ref[0, pl.ds(k0, tk), :]) for r0, k0, _ in tiles]
            lrs = [logs(z, diag) for z, (_, _, diag) in zip(zs, tiles)]
            for (ls, r), (r0, k0, diag) in zip(lrs, tiles):
                carry = [carry_ref[h, r0:, :] for h in range(2)]
                att, carry = attend(ls, r, carry, diag)
                for h in range(2):
                    carry_ref[h, r0:, :] = carry[h]
                acc_ref[r0:, :] += _dot(att, split_heads(v_ref[0, pl.ds(k0, tk), :]))
            return jnp.max(jnp.maximum(carry[0], carry[1]))

        def pair(jj):
            return [(0, pl.multiple_of(q0 - (2 * jj + 1) * tk, tk), False),
                    (0, pl.multiple_of(q0 - (2 * jj + 2) * tk, tk), False)]

        diag_tiles = [(kk * tk, kk * tk if first else pl.multiple_of(q0 + kk * tk, tk), True)
                      for kk in reversed(range(nsub))]
        if first:
            sweep(diag_tiles)
        else:
            def more_keys(state):
                jj, carry_max = state
                return (jj < i * (nsub // 2)) & (carry_max > SB_LOG2_ZERO)

            lax.while_loop(more_keys, lambda state: (state[0] + 1, sweep(pair(state[0]))),
                           (jnp.int32(1), sweep(diag_tiles + pair(0))))
        o_ref[0, pl.ds(q0, tq), :] = acc_ref[...].astype(o_ref.dtype)

    q_tile(0, True)

    def later_q_tile(i, _):
        q_tile(i, False)
        return 0

    lax.fori_loop(1, nq, later_q_tile, 0)


def _sb_attention(qkv, d):
    b, l, _ = qkv.shape
    nhp = d // LANES
    tq = min(SB_TQ, l)
    assert tq % (2 * SB_TK) == 0 and l % tq == 0
    return pl.pallas_call(
        _sb_kernel,
        grid=(b, nhp),
        in_specs=[
            pl.BlockSpec((1, l, LANES), lambda i, j: (i, 0, j)),
            pl.BlockSpec((1, l, LANES), lambda i, j: (i, 0, nhp + j)),
            pl.BlockSpec((1, l, LANES), lambda i, j: (i, 0, 2 * nhp + j)),
        ],
        out_specs=pl.BlockSpec((1, l, LANES), lambda i, j: (i, 0, j)),
        out_shape=jax.ShapeDtypeStruct((b, l, d), BF16),
        scratch_shapes=[pltpu.VMEM((2, tq, LANES), F32), pltpu.VMEM((tq, LANES), F32)],
        compiler_params=_params("parallel", "parallel"),
        name="sb_attention",
    )(qkv, qkv, qkv)


def kernel(x, p, ffn1_norm, ffn1_w1, ffn1_w3, ffn1_w2, mix_norm, ffn2_norm, ffn2_w1, ffn2_w3, ffn2_w2, ple_norm, ple_proj, ple_gate, s5_w_in, s5_a_re, s5_a_im, s5_log_dt, s5_b_re, s5_b_im, s5_c_re, s5_c_im, s5_d, s5_w_glu, sb_w_qkv, sb_w_o, final_norm):
    b, l, d = x.shape
    depth = p.shape[0]
    n = b * l
    bf = lambda w: w.astype(BF16)
    f1w1, f1w3, f1w2 = bf(ffn1_w1), bf(ffn1_w3), bf(ffn1_w2)
    f2w1, f2w3, f2w2 = bf(ffn2_w1), bf(ffn2_w3), bf(ffn2_w2)
    w_gate, w_proj = bf(ple_gate), bf(ple_proj)
    p = p.reshape(depth, n, p.shape[-1])
    last = depth - 1
    h = x.reshape(n, d)
    for i in range(depth):
        if i == 0:
            h = _ffn(h, ffn1_norm[i], f1w1[i], f1w3[i], f1w2[i])
        else:
            h = _ple_ffn(h, p, i - 1, ple_norm[i - 1], w_gate[i - 1], w_proj[i - 1],
                         ffn1_norm[i], f1w1[i], f1w3[i], f1w2[i])
        j = i // 2
        if i % 2 == 0:
            perm = _lane_permutation()
            zg = _s5_in(h, mix_norm[i], bf(s5_w_in[j]), perm, b, l)
            weights = _s5_weights(s5_a_re[j], s5_a_im[j], s5_log_dt[j], s5_b_re[j], s5_b_im[j],
                                  s5_c_re[j], s5_c_im[j], s5_d[j])
            h = _s5_out(h, _s5_core(zg, weights, b, l), bf(s5_w_glu[j]), perm.T)
            h = _ffn(h, ffn2_norm[i], f2w1[i], f2w3[i], f2w2[i])
            if i == last:
                h = _ple_final(h, p, i, ple_norm[i], w_gate[i], w_proj[i], final_norm)
        else:
            col_scale = jnp.where(jnp.arange(3 * d) < d, LOG2_E * SB_HEAD_DIM ** -0.5, 1.0).astype(F32)
            qkv = _norm_mm(h, mix_norm[i], bf(sb_w_qkv[j] * col_scale))
            o = _sb_attention(qkv.reshape(b, l, 3 * d), d)
            ffn2 = (ffn2_norm[i], f2w1[i], f2w3[i], f2w2[i])
            if i == last:
                h = _proj_ffn_final(h, o.reshape(n, d), bf(sb_w_o[j]), *ffn2,
                                    p, i, ple_norm[i], w_gate[i], w_proj[i], final_norm)
            else:
                h = _proj_ffn(h, o.reshape(n, d), bf(sb_w_o[j]), *ffn2)
    return h.reshape(b, l, d)
```

```python
import functools

import jax
import jax.numpy as jnp
from jax import lax
from jax.experimental import pallas as pl
from jax.experimental.pallas import tpu as pltpu

F32 = jnp.float32
BF16 = jnp.bfloat16

EPS = 1e-6
S5_GROUP = 16
S5_STATE = 64
S5_CHUNK = 64
SB_HEAD_DIM = 64
LANES = 128
S5_T_LO = LANES // S5_GROUP
S5_T_HI = S5_CHUNK // S5_T_LO
S5_ROWS = 16
S5_PITCH = S5_CHUNK + 8
SB_TQ = 512
SB_TK = 128
SB_LOG2_ZERO = -150.0
LOG2_E = 1.4426950408889634
FFN_CHUNK = 256
FFN_ROWS = 512
MM_CHUNK = 512
VMEM_LIMIT_BYTES = 56 * 1024 * 1024


def _params(*semantics):
    return pltpu.CompilerParams(dimension_semantics=semantics,
                                vmem_limit_bytes=VMEM_LIMIT_BYTES)


def _rms(x, g):
    return x * lax.rsqrt(jnp.mean(x * x, axis=-1, keepdims=True) + EPS) * g


def _dot(a, b):
    return jnp.dot(a, b, preferred_element_type=F32)


def _swiglu_half_step(x, g_ref, w1_ref, w3_ref, w2_ref, o_ref):
    xn = _rms(x, g_ref[...]).astype(BF16)
    o_ref[...] = x
    dff = w1_ref.shape[2]
    for c in range(0, dff, FFN_CHUNK):
        a = (jax.nn.silu(_dot(xn, w1_ref[0, :, c:c + FFN_CHUNK]))
             * _dot(xn, w3_ref[0, :, c:c + FFN_CHUNK]))
        o_ref[...] += _dot((0.5 * a).astype(BF16), w2_ref[0, c:c + FFN_CHUNK, :])


def _ffn_kernel(x_ref, g_ref, w1_ref, w3_ref, w2_ref, o_ref):
    _swiglu_half_step(x_ref[...], g_ref, w1_ref, w3_ref, w2_ref, o_ref)


def _proj_ffn_kernel(h_ref, a_ref, wa_ref, g_ref, w1_ref, w3_ref, w2_ref, o_ref):
    x = h_ref[...] + _dot(a_ref[...], wa_ref[...])
    _swiglu_half_step(x, g_ref, w1_ref, w3_ref, w2_ref, o_ref)


def _ple_ffn_kernel(h_ref, p_ref, gp_ref, wg_ref, wp_ref, g_ref, w1_ref, w3_ref, w2_ref, o_ref):
    h = h_ref[...]
    gate = jax.nn.sigmoid(_dot(_rms(h, gp_ref[...]).astype(BF16), wg_ref[...]))
    x = h + _dot(p_ref[0].astype(BF16), wp_ref[...]) * gate
    _swiglu_half_step(x, g_ref, w1_ref, w3_ref, w2_ref, o_ref)


def _ple_final_step(h_ref, p_ref, g_ref, wg_ref, wp_ref, fg_ref, o_ref):
    hn = _rms(h_ref[...], g_ref[...]).astype(BF16)
    pb = p_ref[0].astype(BF16)
    for c in range(0, wg_ref.shape[1], MM_CHUNK):
        gate = jax.nn.sigmoid(_dot(hn, wg_ref[:, c:c + MM_CHUNK]))
        o_ref[:, c:c + MM_CHUNK] = h_ref[:, c:c + MM_CHUNK] + _dot(pb, wp_ref[:, c:c + MM_CHUNK]) * gate
    o_ref[...] = _rms(o_ref[...], fg_ref[...])


def _proj_ffn_final_kernel(h_ref, a_ref, wa_ref, g_ref, w1_ref, w3_ref, w2_ref,
                           p_ref, gp_ref, wg_ref, wp_ref, fg_ref, o_ref):
    x = h_ref[...] + _dot(a_ref[...], wa_ref[...])
    _swiglu_half_step(x, g_ref, w1_ref, w3_ref, w2_ref, o_ref)
    _ple_final_step(o_ref, p_ref, gp_ref, wg_ref, wp_ref, fg_ref, o_ref)


def _resident(shape):
    return pl.BlockSpec(shape, lambda *_: (0,) * len(shape), pipeline_mode=pl.Buffered(1))


def _layer_slab(w, layer):
    return pl.BlockSpec((1,) + w.shape[1:], lambda *_: (layer, 0, 0), pipeline_mode=pl.Buffered(1))


def _ffn_call(body, lead, lead_specs, g, ffn_w, name, tail=(), tail_specs=()):
    w1, w3, w2, layer = ffn_w
    n, d = lead[0].shape
    assert w1.shape[2] % FFN_CHUNK == 0
    return pl.pallas_call(
        body,
        grid=(n // FFN_ROWS,),
        in_specs=(list(lead_specs) + [_resident((1, d))]
                  + [_layer_slab(w, layer) for w in (w1, w3, w2)] + list(tail_specs)),
        out_specs=pl.BlockSpec((FFN_ROWS, d), lambda i: (i, 0)),
        out_shape=jax.ShapeDtypeStruct((n, d), F32),
        compiler_params=_params("parallel"),
        name=name,
    )(*lead, g.reshape(1, d), w1, w3, w2, *tail)


def _rows(width):
    return pl.BlockSpec((FFN_ROWS, width), lambda i: (i, 0))


def _ffn(x, g, ffn_w):
    return _ffn_call(_ffn_kernel, [x], [_rows(x.shape[1])], g, ffn_w, "ffn")


def _proj_ffn(h, a, wa, g, ffn_w):
    return _ffn_call(_proj_ffn_kernel, [h, a, wa],
                     [_rows(h.shape[1]), _rows(a.shape[1]), _resident(wa.shape)],
                     g, ffn_w, "proj_ffn")


def _proj_ffn_final(h, a, wa, g, ffn_w, p, layer, gp, w_gate, w_proj, final_g):
    d = h.shape[1]
    p_spec = pl.BlockSpec((1, FFN_ROWS, p.shape[2]), lambda i: (layer, i, 0))
    return _ffn_call(_proj_ffn_final_kernel, [h, a, wa],
                     [_rows(d), _rows(a.shape[1]), _resident(wa.shape)],
                     g, ffn_w, "proj_ffn_final",
                     tail=[p, gp.reshape(1, d), w_gate, w_proj, final_g.reshape(1, d)],
                     tail_specs=[p_spec, _resident((1, d)), _resident(w_gate.shape),
                                 _resident(w_proj.shape), _resident((1, d))])


def _ple_ffn(h, p, layer, gp, w_gate, w_proj, g, ffn_w):
    d = h.shape[1]
    p_spec = pl.BlockSpec((1, FFN_ROWS, p.shape[2]), lambda i: (layer, i, 0))
    return _ffn_call(_ple_ffn_kernel, [h, p, gp.reshape(1, d), w_gate, w_proj],
                     [_rows(d), p_spec, _resident((1, d)), _resident(w_gate.shape),
                      _resident(w_proj.shape)],
                     g, ffn_w, "ple_ffn")


def _norm_mm_kernel(x_ref, g_ref, w_ref, o_ref):
    xn = _rms(x_ref[...], g_ref[...]).astype(BF16)
    for c in range(0, w_ref.shape[1], MM_CHUNK):
        o_ref[:, c:c + MM_CHUNK] = _dot(xn, w_ref[:, c:c + MM_CHUNK]).astype(o_ref.dtype)


def _norm_mm(x, g, w):
    n, d = x.shape
    dout = w.shape[1]
    tm = min(512, n)
    assert dout % MM_CHUNK == 0
    return pl.pallas_call(
        _norm_mm_kernel,
        grid=(n // tm,),
        in_specs=[
            pl.BlockSpec((tm, d), lambda i: (i, 0)),
            _resident((1, d)),
            _resident((d, dout)),
        ],
        out_specs=pl.BlockSpec((tm, dout), lambda i: (i, 0)),
        out_shape=jax.ShapeDtypeStruct((n, dout), BF16),
        compiler_params=_params("parallel"),
        name="norm_mm",
    )(x, g.reshape(1, d), w)


def _lane_permutation():
    a = jnp.arange(S5_T_LO * LANES)
    t_lo, g8, h = a // LANES, (a // S5_GROUP) % S5_T_LO, a % S5_GROUP
    b = g8 * LANES + t_lo * S5_GROUP + h
    return (b[:, None] == a[None, :]).astype(BF16)


def _frame_rows(j, t_hi, t_lo, rb):
    return pl.ds(j * rb * S5_PITCH + t_hi * S5_T_LO + t_lo, rb, stride=S5_PITCH)


def _chunk_rows(j, r, rb):
    return pl.ds((j * rb + r) * S5_PITCH, S5_CHUNK)


def _s5_in_kernel(x_ref, g_ref, w_ref, perm_ref, z_ref, xs_ref):
    rb = z_ref.shape[2]
    nj = x_ref.shape[1] // LANES
    blk = S5_T_HI * rb
    for j in range(nj):
        for r in range(rb):
            xs_ref[_chunk_rows(j, r, rb), :] = x_ref[r * S5_CHUNK:(r + 1) * S5_CHUNK,
                                                     j * LANES:(j + 1) * LANES]
    xperm = jnp.concatenate(
        [jnp.concatenate([xs_ref[_frame_rows(j, th, tl, rb), :] for j in range(nj)], axis=1)
         for tl in range(S5_T_LO) for th in range(S5_T_HI)], axis=0)
    u = _dot(_rms(xperm, g_ref[...]).astype(BF16), w_ref[...]).astype(BF16)
    lhs = jnp.concatenate(
        [jnp.concatenate([u[tl * blk:(tl + 1) * blk, j * LANES:(j + 1) * LANES]
                          for tl in range(S5_T_LO)], axis=1) for j in range(nj)], axis=0)
    zall = _dot(lhs, perm_ref[...]).astype(BF16)
    for j in range(nj):
        for th in range(S5_T_HI):
            r0 = (j * S5_T_HI + th) * rb
            for g8 in range(S5_T_LO):
                z_ref[j * S5_T_LO + g8, th, :, :] = zall[r0:r0 + rb, g8 * LANES:(g8 + 1) * LANES]


def _s5_in(x, g, w, perm, b, l):
    n, d = x.shape
    rb = S5_ROWS
    tm = rb * S5_CHUNK
    groups = d // S5_GROUP
    chunks = n // S5_CHUNK
    assert n % tm == 0 and l % S5_CHUNK == 0 and d % LANES == 0
    return pl.pallas_call(
        _s5_in_kernel,
        grid=(n // tm,),
        in_specs=[
            pl.BlockSpec((tm, d), lambda i: (i, 0)),
            _resident((1, d)),
            _resident(w.shape),
            _resident(perm.shape),
        ],
        out_specs=pl.BlockSpec((groups, S5_T_HI, rb, LANES), lambda i: (0, 0, i, 0)),
        out_shape=jax.ShapeDtypeStruct((groups, S5_T_HI, chunks, LANES), BF16),
        scratch_shapes=[pltpu.VMEM((d // LANES * rb * S5_PITCH, LANES), F32)],
        compiler_params=_params("parallel"),
        name="s5_in",
    )(x, g.reshape(1, d), w, perm)


def _s5_out_kernel(h_ref, z_ref, w_ref, permt_ref, o_ref, us_ref):
    rb = z_ref.shape[2]
    tm, d = h_ref.shape
    nj = d // LANES
    blk = S5_T_HI * rb
    lhs = jnp.concatenate(
        [jnp.concatenate([z_ref[j * S5_T_LO + g8, th] for g8 in range(S5_T_LO)], axis=1)
         for j in range(nj) for th in range(S5_T_HI)], axis=0)
    zp = _dot(lhs, permt_ref[...]).astype(BF16)
    z = jnp.concatenate(
        [jnp.concatenate([zp[j * blk:(j + 1) * blk, tl * LANES:(tl + 1) * LANES]
                          for j in range(nj)], axis=1) for tl in range(S5_T_LO)], axis=0)
    for c in range(0, d, MM_CHUNK):
        out = _dot(z, w_ref[:, c:c + MM_CHUNK])
        gate = _dot(z, w_ref[:, d + c:d + c + MM_CHUNK])
        upd = out * jax.nn.sigmoid(gate)
        for tl in range(S5_T_LO):
            for th in range(S5_T_HI):
                r0 = (tl * S5_T_HI + th) * rb
                for jj in range(MM_CHUNK // LANES):
                    us_ref[_frame_rows(c // LANES + jj, th, tl, rb), :] = (
                        upd[r0:r0 + rb, jj * LANES:(jj + 1) * LANES])
    for j in range(nj):
        for r in range(rb):
            rows = slice(r * S5_CHUNK, (r + 1) * S5_CHUNK)
            lanes = slice(j * LANES, (j + 1) * LANES)
            o_ref[rows, lanes] = h_ref[rows, lanes] + us_ref[_chunk_rows(j, r, rb), :]


def _s5_out(h, zg, w, permt):
    n, d = h.shape
    rb = S5_ROWS
    tm = rb * S5_CHUNK
    return pl.pallas_call(
        _s5_out_kernel,
        grid=(n // tm,),
        in_specs=[
            pl.BlockSpec((tm, d), lambda i: (i, 0)),
            pl.BlockSpec((zg.shape[0], S5_T_HI, rb, LANES), lambda i: (0, 0, i, 0)),
            _resident(w.shape),
            _resident(permt.shape),
        ],
        out_specs=pl.BlockSpec((tm, d), lambda i: (i, 0)),
        out_shape=jax.ShapeDtypeStruct((n, d), F32),
        scratch_shapes=[pltpu.VMEM((d // LANES * rb * S5_PITCH, LANES), F32)],
        compiler_params=_params("parallel"),
        name="s5_out",
    )(h, zg, w, permt)


def _ple_final(h, p, layer, g, w_gate, w_proj, final_g):
    n, d = h.shape
    tm = min(512, n)
    return pl.pallas_call(
        _ple_final_step,
        grid=(n // tm,),
        in_specs=[
            pl.BlockSpec((tm, d), lambda i: (i, 0)),
            pl.BlockSpec((1, tm, p.shape[2]), lambda i: (layer, i, 0)),
            _resident((1, d)),
            _resident(w_gate.shape),
            _resident(w_proj.shape),
            _resident((1, d)),
        ],
        out_specs=pl.BlockSpec((tm, d), lambda i: (i, 0)),
        out_shape=jax.ShapeDtypeStruct((n, d), F32),
        compiler_params=_params("parallel"),
        name="ple_final",
    )(h, p, g.reshape(1, d), w_gate, w_proj, final_g.reshape(1, d))


def _s5_kernel(x_ref, bc_ref, pw_ref, dec_ref, d_ref, z_ref,
               m_ref, bm_ref, cw_ref, sin_ref, sw_ref, sp_ref, *, nb, nc):
    hh = bc_ref.shape[2]
    t = m_ref.shape[0] // hh
    ns = sp_ref.shape[1] // 2

    b1, b2, c1, c2 = bc_ref[0, 0], bc_ref[0, 1], bc_ref[0, 2], bc_ref[0, 3]
    for k in range(t + 1):
        cw_ref[k * hh:(k + 1) * hh, :] = c1 * pw_ref[0, 0, k:k + 1, :] + c2 * pw_ref[0, 1, k:k + 1, :]
    for s in range(t):
        k = t - 1 - s
        bm_ref[s * hh:(s + 1) * hh, :] = (
            b1 * pw_ref[0, 0, k:k + 1, :] + b2 * pw_ref[0, 1, k:k + 1, :]).astype(BF16)

    taps = lax.dot_general(b1, cw_ref[0:t * hh, :], (((1,), (1,)), ((), ())),
                           precision=lax.Precision.HIGHEST, preferred_element_type=F32)
    lane = lax.broadcasted_iota(jnp.int32, taps.shape, 1)
    m_ref[0:hh, :] = taps.astype(BF16)
    for s in range(1, t):
        shifted = jnp.where(lane >= s * hh, pltpu.roll(taps, s * hh, axis=1), 0.0)
        m_ref[s * hh:(s + 1) * hh, :] = shifted.astype(BF16)
    cmat = cw_ref[hh:(t + 1) * hh, :].astype(BF16)

    x = jnp.concatenate([x_ref[0, th] for th in range(S5_T_HI)], axis=1)
    s_in = _dot(x, bm_ref[...])
    s_in_sw = pltpu.roll(s_in, ns, axis=1)
    pitch = sp_ref.shape[0] // nb
    for b in range(nb):
        sin_ref[b * pitch:b * pitch + nc, :] = s_in[b * nc:(b + 1) * nc]
        sw_ref[b * pitch:b * pitch + nc, :] = s_in_sw[b * nc:(b + 1) * nc]
    dec = dec_ref[0]
    a_rr, a_s, a_sw = dec[0:1], dec[1:2], dec[2:3]
    st = jnp.zeros((nb, 2 * ns), F32)
    sw = jnp.zeros((nb, 2 * ns), F32)
    for c in range(nc):
        rows = pl.ds(c, nb, stride=pitch)
        sp_ref[rows, :] = st
        st, sw = (st * a_rr + sw * a_s + sin_ref[rows, :],
                  sw * a_rr + st * a_sw + sw_ref[rows, :])
    s_prev = jnp.concatenate([sp_ref[b * pitch:b * pitch + nc, :] for b in range(nb)], axis=0)

    y = _dot(x, m_ref[...]) + lax.dot_general(
        s_prev.astype(BF16), cmat, (((1,), (1,)), ((), ())), preferred_element_type=F32)
    y = y + d_ref[0] * x.astype(F32)
    z = jax.nn.gelu(y).astype(BF16)
    for th in range(S5_T_HI):
        z_ref[0, th] = z[:, th * LANES:(th + 1) * LANES]


def _s5_weights(a_re, a_im, log_dt, b_re, b_im, c_re, c_im, d_skip):
    t = S5_CHUNK
    g, _, hh = b_re.shape
    lam_re = jnp.minimum(a_re.astype(F32), -1e-4)
    lam_im = a_im.astype(F32)
    dt = jnp.exp(log_dt.astype(F32))[:, None]
    mag = jnp.exp(lam_re * dt)
    abar_re = mag * jnp.cos(lam_im * dt)
    abar_im = mag * jnp.sin(lam_im * dt)
    den = lam_re * lam_re + lam_im * lam_im
    nr = abar_re - 1.0
    ni = abar_im
    fr = (nr * lam_re + ni * lam_im) / den
    fi = (ni * lam_re - nr * lam_im) / den
    bre = b_re.astype(F32)
    bim = b_im.astype(F32)
    bbar_re = fr[..., None] * bre - fi[..., None] * bim
    bbar_im = fr[..., None] * bim + fi[..., None] * bre
    cre = c_re.astype(F32)
    cim = c_im.astype(F32)

    k = jnp.arange(t + 1, dtype=F32)[None, :, None]
    pmag = jnp.exp(k * (lam_re * dt)[:, None, :])
    pw_re = pmag * jnp.cos(k * (lam_im * dt)[:, None, :])
    pw_im = pmag * jnp.sin(k * (lam_im * dt)[:, None, :])
    bt_re = bbar_re.transpose(0, 2, 1)
    bt_im = bbar_im.transpose(0, 2, 1)

    pack = lambda a, b_: jnp.concatenate([a, b_], axis=-1)
    bc = jnp.stack([pack(bt_re, bt_im), pack(-bt_im, bt_re),
                    pack(cre, -cim), pack(-cim, -cre)], axis=1)
    pw2 = jnp.stack([pack(pw_re, pw_re), pack(pw_im, pw_im)], axis=1)

    at_re, at_im = pw_re[:, t], pw_im[:, t]
    dec = jnp.stack([
        jnp.concatenate([at_re, at_re], axis=1),
        jnp.concatenate([-at_im, at_im], axis=1),
        jnp.concatenate([at_im, -at_im], axis=1),
    ], axis=1)
    dec = jnp.pad(dec, ((0, 0), (0, 5), (0, 0)))

    dflat = jnp.tile(d_skip.astype(F32).reshape(g, 1, hh), (1, t, 1)).reshape(g, 1, t * hh)
    return bc, pw2, dec, dflat


def _s5_core(zg, weights, b, l):
    bc, pw2, dec, dflat = weights
    g, hh, ns2 = bc.shape[0], bc.shape[2], bc.shape[3]
    nc = l // S5_CHUNK
    rows = nc * b
    th = S5_CHUNK * hh
    blk = (1, S5_T_HI, rows, LANES)
    return pl.pallas_call(
        functools.partial(_s5_kernel, nb=b, nc=nc),
        grid=(g,),
        in_specs=[
            pl.BlockSpec(blk, lambda i: (i, 0, 0, 0)),
            pl.BlockSpec((1,) + bc.shape[1:], lambda i: (i, 0, 0, 0)),
            pl.BlockSpec((1,) + pw2.shape[1:], lambda i: (i, 0, 0, 0)),
            pl.BlockSpec((1, 8, ns2), lambda i: (i, 0, 0)),
            pl.BlockSpec((1, 1, th), lambda i: (i, 0, 0)),
        ],
        out_specs=pl.BlockSpec(blk, lambda i: (i, 0, 0, 0)),
        out_shape=jax.ShapeDtypeStruct(zg.shape, BF16),
        scratch_shapes=([pltpu.VMEM((th, th), BF16), pltpu.VMEM((th, ns2), BF16),
                         pltpu.VMEM((th + hh, ns2), F32)]
                        + [pltpu.VMEM((b * (nc + 8), ns2), F32)] * 3),
        compiler_params=_params("parallel"),
        name="s5_core",
    )(zg, bc, pw2, dec, dflat)


def _sb_kernel(q_ref, k_ref, v_ref, o_ref, carry_ref, acc_ref):
    tk = SB_TK
    tq = carry_ref.shape[1]
    nsub = tq // tk
    nq = q_ref.shape[1] // tq
    two = 2 * LANES

    lane = lax.broadcasted_iota(jnp.int32, (tk, two), 1)
    row = lax.broadcasted_iota(jnp.int32, (tk, two), 0)
    strict = (lane & (LANES - 1)) < row
    trow = lax.broadcasted_iota(jnp.int32, (two, two), 0)
    tcol = lax.broadcasted_iota(jnp.int32, (two, two), 1)
    tri = jnp.where((tcol >= LANES) | ((trow & (LANES - 1)) > tcol), -1.0, 0.0).astype(BF16)
    head0 = lax.broadcasted_iota(jnp.int32, (tk, LANES), 1) < SB_HEAD_DIM

    def split_heads(t):
        zero = jnp.zeros_like(t)
        return jnp.concatenate([jnp.where(head0, t, zero), jnp.where(head0, zero, t)], axis=0)

    def mask_diag(a):
        top = jnp.where(strict, a[:tk], 0.0)
        return top if a.shape[0] == tk else jnp.concatenate([top, a[tk:]], axis=0)

    def qk(q_rows, kb):
        return lax.dot_general(q_rows, split_heads(kb), (((1,), (1,)), ((), ())),
                               preferred_element_type=F32)

    def logs(z, diag):
        nlk = jnp.maximum(z, 0.0) + jnp.log2(1.0 + jnp.exp2(-jnp.abs(z)))
        ls = z - nlk
        if diag:
            nlk = mask_diag(nlk)
        hi = nlk.astype(BF16)
        lo = (nlk - hi.astype(F32)).astype(BF16)
        r = [_dot(jnp.concatenate([hi[:, h * LANES:(h + 1) * LANES],
                                   lo[:, h * LANES:(h + 1) * LANES]], axis=1), tri)
             for h in range(2)]
        return ls, r

    def attend(ls, r, carry, diag):
        arg = jnp.concatenate([r[h][:, :LANES] + carry[h] for h in range(2)], axis=1) + ls
        att = jnp.exp2(arg)
        if diag:
            att = mask_diag(att)
        return att.astype(BF16), [carry[h] + r[h][:, LANES:] for h in range(2)]

    def q_tile(i, first):
        q0 = 0 if first else pl.multiple_of(i * tq, tq)
        q = q_ref[0, pl.ds(q0, tq), :]
        carry_ref[...] = jnp.zeros_like(carry_ref)
        acc_ref[...] = jnp.zeros_like(acc_ref)

        def sweep(tiles):
            zs = [qk(q[r0:], k_ref[0, pl.ds(k0, tk), :]) for r0, k0, _ in tiles]
            lrs = [logs(z, diag) for z, (_, _, diag) in zip(zs, tiles)]
            for (ls, r), (r0, k0, diag) in zip(lrs, tiles):
                carry = [carry_ref[h, r0:, :] for h in range(2)]
                att, carry = attend(ls, r, carry, diag)
                for h in range(2):
                    carry_ref[h, r0:, :] = carry[h]
                acc_ref[r0:, :] += _dot(att, split_heads(v_ref[0, pl.ds(k0, tk), :]))
            return jnp.max(jnp.maximum(carry[0], carry[1]))

        def pair(jj):
            return [(0, pl.multiple_of(q0 - (2 * jj + 1) * tk, tk), False),
                    (0, pl.multiple_of(q0 - (2 * jj + 2) * tk, tk), False)]

        diag_tiles = [(kk * tk, kk * tk if first else pl.multiple_of(q0 + kk * tk, tk), True)
                      for kk in reversed(range(nsub))]
        if first:
            sweep(diag_tiles)
        else:
            def more_keys(state):
                jj, carry_max = state
                return (jj < i * (nsub // 2)) & (carry_max > SB_LOG2_ZERO)

            lax.while_loop(more_keys, lambda state: (state[0] + 1, sweep(pair(state[0]))),
                           (jnp.int32(1), sweep(diag_tiles + pair(0))))
        o_ref[0, pl.ds(q0, tq), :] = acc_ref[...].astype(o_ref.dtype)

    q_tile(0, True)

    def later_q_tile(i, _):
        q_tile(i, False)
        return 0

    lax.fori_loop(1, nq, later_q_tile, 0)


def _sb_attention(qkv, d):
    b, l, _ = qkv.shape
    nhp = d // LANES
    tq = min(SB_TQ, l)
    assert tq % (2 * SB_TK) == 0 and l % tq == 0
    return pl.pallas_call(
        _sb_kernel,
        grid=(b, nhp),
        in_specs=[
            pl.BlockSpec((1, l, LANES), lambda i, j: (i, 0, j)),
            pl.BlockSpec((1, l, LANES), lambda i, j: (i, 0, nhp + j)),
            pl.BlockSpec((1, l, LANES), lambda i, j: (i, 0, 2 * nhp + j)),
        ],
        out_specs=pl.BlockSpec((1, l, LANES), lambda i, j: (i, 0, j)),
        out_shape=jax.ShapeDtypeStruct((b, l, d), BF16),
        scratch_shapes=[pltpu.VMEM((2, tq, LANES), F32), pltpu.VMEM((tq, LANES), F32)],
        compiler_params=_params("parallel", "parallel"),
        name="sb_attention",
    )(qkv, qkv, qkv)


def kernel(x, p, ffn1_norm, ffn1_w1, ffn1_w3, ffn1_w2, mix_norm, ffn2_norm, ffn2_w1, ffn2_w3, ffn2_w2, ple_norm, ple_proj, ple_gate, s5_w_in, s5_a_re, s5_a_im, s5_log_dt, s5_b_re, s5_b_im, s5_c_re, s5_c_im, s5_d, s5_w_glu, sb_w_qkv, sb_w_o, final_norm):
    b, l, d = x.shape
    depth = p.shape[0]
    n = b * l
    bf = lambda w: w.astype(BF16)
    f1w1, f1w3, f1w2 = bf(ffn1_w1), bf(ffn1_w3), bf(ffn1_w2)
    f2w1, f2w3, f2w2 = bf(ffn2_w1), bf(ffn2_w3), bf(ffn2_w2)
    w_gate, w_proj = bf(ple_gate), bf(ple_proj)
    p = p.reshape(depth, n, p.shape[-1])
    last = depth - 1
    h = x.reshape(n, d)
    for i in range(depth):
        ffn1 = (ffn1_norm[i], (f1w1, f1w3, f1w2, i))
        ffn2 = (ffn2_norm[i], (f2w1, f2w3, f2w2, i))
        if i == 0:
            h = _ffn(h, *ffn1)
        else:
            h = _ple_ffn(h, p, i - 1, ple_norm[i - 1], w_gate[i - 1], w_proj[i - 1], *ffn1)
        j = i // 2
        if i % 2 == 0:
            perm = _lane_permutation()
            zg = _s5_in(h, mix_norm[i], bf(s5_w_in[j]), perm, b, l)
            weights = _s5_weights(s5_a_re[j], s5_a_im[j], s5_log_dt[j], s5_b_re[j], s5_b_im[j],
                                  s5_c_re[j], s5_c_im[j], s5_d[j])
            h = _s5_out(h, _s5_core(zg, weights, b, l), bf(s5_w_glu[j]), perm.T)
            h = _ffn(h, *ffn2)
            if i == last:
                h = _ple_final(h, p, i, ple_norm[i], w_gate[i], w_proj[i], final_norm)
        else:
            col_scale = jnp.where(jnp.arange(3 * d) < d, LOG2_E * SB_HEAD_DIM ** -0.5, 1.0).astype(F32)
            qkv = _norm_mm(h, mix_norm[i], bf(sb_w_qkv[j] * col_scale))
            o = _sb_attention(qkv.reshape(b, l, 3 * d), d)
            if i == last:
                h = _proj_ffn_final(h, o.reshape(n, d), bf(sb_w_o[j]), *ffn2,
                                    p, i, ple_norm[i], w_gate[i], w_proj[i], final_norm)
            else:
                h = _proj_ffn(h, o.reshape(n, d), bf(sb_w_o[j]), *ffn2)
    return h.reshape(b, l, d)
```
